```python
import math
import jax, jax.numpy as jnp
from jax import lax
import numpy as np

D_MODEL = 1024
BATCH = 8
SEQ = 4096
DEPTH = 1
DEC_BATCH = 32
DEC_SEQ = 8
PAST_LEN = 16384
PAGE_SIZE = 128

GLA_HEADS = 4
GLA_DV = D_MODEL // (2 * GLA_HEADS)
GLA_DK = GLA_DV // 2
GLA_GATE_RANK = 16
GLA_GATE_NORMALIZER = 16.0
GLA_CHUNK = 64
DIFF_HEADS = 4
DIFF_DV = D_MODEL // (2 * DIFF_HEADS)
DIFF_DH = DIFF_DV // 2
ROPE_THETA = 10000.0
Q_BLOCK = 128
_FF_RAW = -(-8 * D_MODEL // 3)
D_FF = -(-_FF_RAW // 256) * 256
PLE_DIM = 256
EPS = 1e-6
GLA_WIDTH = GLA_HEADS * GLA_DV
DIFF_WIDTH = DIFF_HEADS * DIFF_DV
MIX_WIDTH = GLA_WIDTH + DIFF_WIDTH
IN_SIZES = (GLA_HEADS * GLA_DK, GLA_HEADS * GLA_DK, GLA_WIDTH, GLA_WIDTH, GLA_GATE_RANK,
            DIFF_HEADS * 2 * DIFF_DH, DIFF_HEADS * 2 * DIFF_DH, DIFF_WIDTH)
IN_WIDTH = sum(IN_SIZES)

kernel_name = 'hybrid_gla_diffattn_step'

F32 = jnp.float32


def rmsnorm(x, g):
    xf = x.astype(F32)
    y = xf * lax.rsqrt(jnp.mean(xf * xf, axis=-1, keepdims=True) + EPS)
    return (y * g.astype(F32)).astype(x.dtype)


def rope(x, pos):
    half = x.shape[-1] // 2
    inv = ROPE_THETA ** (-jnp.arange(half, dtype=F32) / half)
    ang = pos.astype(F32)[:, None] * inv[None, :]
    cos = jnp.cos(ang)[:, None, None, :]
    sin = jnp.sin(ang)[:, None, None, :]
    xf = x.astype(F32)
    x1, x2 = xf[..., :half], xf[..., half:]
    return jnp.concatenate([x1 * cos - x2 * sin, x2 * cos + x1 * sin], axis=-1).astype(x.dtype)


def lambda_init_fn(layer_idx):
    return 0.8 - 0.6 * math.exp(-0.3 * layer_idx)


def diff_lambda(lam_params, lam_init):
    lp = lam_params.astype(F32)
    return jnp.exp(jnp.sum(lp[0] * lp[1])) - jnp.exp(jnp.sum(lp[2] * lp[3])) + lam_init


def mixer_inputs(xn, pos, w_in, w_gk2, b_gk, q_norm_g, k_norm_g):
    B, T, _ = xn.shape
    proj = xn @ w_in
    splits = [int(s) for s in np.cumsum(IN_SIZES)[:-1]]
    gq, gk, gv, gg, glr, dq, dk, dv = jnp.split(proj, splits, axis=-1)
    gq = gq.reshape(B, T, GLA_HEADS, GLA_DK) * (GLA_DK ** -0.5)
    gk = gk.reshape(B, T, GLA_HEADS, GLA_DK)
    gv = gv.reshape(B, T, GLA_HEADS, GLA_DV)
    log_a = jax.nn.log_sigmoid((glr @ w_gk2 + b_gk).astype(F32)) / GLA_GATE_NORMALIZER
    log_a = log_a.reshape(B, T, GLA_HEADS, GLA_DK)
    dq = rope(rmsnorm(dq.reshape(B, T, DIFF_HEADS, 2, DIFF_DH), q_norm_g), pos)
    dk = rope(rmsnorm(dk.reshape(B, T, DIFF_HEADS, 2, DIFF_DH), k_norm_g), pos)
    dv = dv.reshape(B, T, DIFF_HEADS, DIFF_DV)
    return (gq, gk, gv, log_a, gg), (dq, dk, dv)


def gla_chunked(q, k, v, log_a, s0):
    B, T, H, DK = q.shape
    DV = v.shape[-1]
    C = min(GLA_CHUNK, T)
    n = -(-T // C)
    pad = n * C - T

    def blocks(a):
        a = jnp.pad(a, ((0, 0), (0, pad), (0, 0), (0, 0)))
        return a.reshape(B, n, C, H, a.shape[-1]).transpose(1, 0, 3, 2, 4)

    causal = jnp.tril(jnp.ones((C, C), dtype=bool))[:, :, None]

    def step(S, blk):
        qb, kb, vb, ab = blk
        qf, kf, vf = qb.astype(F32), kb.astype(F32), vb.astype(F32)
        b = jnp.cumsum(ab.astype(F32), axis=2)
        rel = jnp.where(causal, b[:, :, :, None, :] - b[:, :, None, :, :], -jnp.inf)
        A = jnp.sum(qf[:, :, :, None, :] * kf[:, :, None, :, :] * jnp.exp(rel), axis=-1)
        o = (jnp.einsum('bhts,bhsv->bhtv', A, vf)
             + jnp.einsum('bhtk,bhkv->bhtv', qf * jnp.exp(b), S))
        b_last = b[:, :, -1:, :]
        S = (jnp.exp(b_last[:, :, 0, :])[..., None] * S
             + jnp.einsum('bhsk,bhsv->bhkv', kf * jnp.exp(b_last - b), vf))
        return S, o

    S, o = lax.scan(step, s0.astype(F32), tuple(blocks(a) for a in (q, k, v, log_a)))
    o = o.transpose(1, 0, 3, 2, 4).reshape(B, n * C, H, DV)[:, :T]
    return o.astype(v.dtype), S.astype(s0.dtype)


def diff_attend_prompt(q, k, v, lam):
    B, S, H, _, DH = q.shape
    nb = S // Q_BLOCK
    qb = q.reshape(B, nb, Q_BLOCK, H, 2, DH).transpose(1, 0, 2, 3, 4, 5)
    kpos = jnp.arange(S)
    scale = DH ** -0.5

    def blk(args):
        i, qi = args
        s = jnp.einsum('bqhcd,bkhcd->bhcqk', qi, k).astype(F32) * scale
        qpos = i * Q_BLOCK + jnp.arange(Q_BLOCK)
        s = jnp.where(kpos[None, :] <= qpos[:, None], s, -jnp.inf)
        a = jax.nn.softmax(s, axis=-1)
        w = a[:, :, 0] - lam * a[:, :, 1]
        return jnp.einsum('bhqk,bkhv->bqhv', w.astype(v.dtype), v)

    o = lax.map(blk, (jnp.arange(nb), qb))
    return o.transpose(1, 0, 2, 3, 4).reshape(B, S, H, v.shape[-1])


def diff_attend_sample(q, k, v, lam, k_past, v_past):
    T = q.shape[1]
    P = k_past.shape[1]
    scale = q.shape[-1] ** -0.5
    s_past = jnp.einsum('bqhcd,bkhcd->bhcqk', q, k_past).astype(F32) * scale
    s_new = jnp.einsum('bqhcd,bkhcd->bhcqk', q, k).astype(F32) * scale
    s_new = jnp.where(jnp.tril(jnp.ones((T, T), dtype=bool)), s_new, -jnp.inf)
    a = jax.nn.softmax(jnp.concatenate([s_past, s_new], axis=-1), axis=-1)
    w = (a[:, :, 0] - lam * a[:, :, 1]).astype(v.dtype)
    return (jnp.einsum('bhqk,bkhv->bqhv', w[..., :P], v_past)
            + jnp.einsum('bhqk,bkhv->bqhv', w[..., P:], v))


def mixer_output(gla_o, gla_g, diff_o, lam_init, w_out, gla_norm_g, diff_norm_g):
    B, T = gla_o.shape[:2]
    go = rmsnorm(gla_o, gla_norm_g).reshape(B, T, GLA_WIDTH) * jax.nn.silu(gla_g)
    do = (rmsnorm(diff_o, diff_norm_g) * (1.0 - lam_init)).reshape(B, T, DIFF_WIDTH)
    return jnp.concatenate([go, do], axis=-1) @ w_out


def trunk_layer(x, p, pos, gla_s0, attend, lam_init, attn_norm_g, w_in, w_gk2, b_gk,
                q_norm_g, k_norm_g, lam_params, gla_norm_g, diff_norm_g, w_out,
                ffn_norm_g, w_ffn_gate, w_ffn_up, w_ffn_down, ple_norm_g, w_ple_gate, w_ple_proj):
    xn = rmsnorm(x, attn_norm_g)
    (gq, gk, gv, log_a, gg), (dq, dk, dv) = mixer_inputs(xn, pos, w_in, w_gk2, b_gk,
                                                        q_norm_g, k_norm_g)
    gla_o, gla_s = gla_chunked(gq, gk, gv, log_a, gla_s0)
    lam = diff_lambda(lam_params, lam_init)
    diff_o = attend(dq, dk, dv, lam)
    h = x + mixer_output(gla_o, gg, diff_o, lam_init, w_out, gla_norm_g, diff_norm_g)
    hn = rmsnorm(h, ffn_norm_g)
    h = h + (jax.nn.silu(hn @ w_ffn_gate) * (hn @ w_ffn_up)) @ w_ffn_down
    gate = jax.nn.sigmoid(rmsnorm(h, ple_norm_g) @ w_ple_gate)
    h = h + gate * (p @ w_ple_proj)
    return h, dk, dv, gla_s


def setup_inputs(seed: int = 0) -> dict:
    key = jax.random.key(seed)
    ks = iter(jax.random.split(key, 40))
    n_pages = PAST_LEN // PAGE_SIZE
    n_pool = (DEC_BATCH * n_pages * 5) // 4

    def nrm(shape, scale):
        return jax.random.normal(next(ks), shape, F32) * scale

    def gain(shape):
        return 1.0 + 0.05 * jax.random.normal(next(ks), shape, F32)

    perm = jax.random.permutation(next(ks), n_pool)
    page_table = perm[:DEC_BATCH * n_pages].reshape(DEC_BATCH, n_pages).astype(jnp.int32)
    return {
        'x_prompt': nrm((BATCH, SEQ, D_MODEL), 1.0),
        'x_sample': nrm((DEC_BATCH, DEC_SEQ, D_MODEL), 1.0),
        'p_prompt': nrm((DEPTH, BATCH, SEQ, PLE_DIM), 1.0),
        'p_sample': nrm((DEPTH, DEC_BATCH, DEC_SEQ, PLE_DIM), 1.0),
        'cache_k': nrm((DEPTH, n_pool, PAGE_SIZE, DIFF_HEADS, 2, DIFF_DH), 1.0),
        'cache_v': nrm((DEPTH, n_pool, PAGE_SIZE, DIFF_HEADS, DIFF_DV), 1.0),
        'state_gla': nrm((DEPTH, DEC_BATCH, GLA_HEADS, GLA_DK, GLA_DV), 1.0),
        'page_table': page_table,
        'attn_norm_g': gain((DEPTH, D_MODEL)),
        'w_in': nrm((DEPTH, D_MODEL, IN_WIDTH), D_MODEL ** -0.5),
        'w_gk2': nrm((DEPTH, GLA_GATE_RANK, GLA_HEADS * GLA_DK), GLA_GATE_RANK ** -0.5),
        'b_gk': nrm((DEPTH, GLA_HEADS * GLA_DK), 0.1),
        'q_norm_g': gain((DEPTH, DIFF_DH)),
        'k_norm_g': gain((DEPTH, DIFF_DH)),
        'lam_params': nrm((DEPTH, 4, DIFF_DH), 0.1),
        'gla_norm_g': gain((DEPTH, GLA_DV)),
        'diff_norm_g': gain((DEPTH, DIFF_DV)),
        'w_out': nrm((DEPTH, MIX_WIDTH, D_MODEL), MIX_WIDTH ** -0.5),
        'ffn_norm_g': gain((DEPTH, D_MODEL)),
        'w_ffn_gate': nrm((DEPTH, D_MODEL, D_FF), D_MODEL ** -0.5),
        'w_ffn_up': nrm((DEPTH, D_MODEL, D_FF), D_MODEL ** -0.5),
        'w_ffn_down': nrm((DEPTH, D_FF, D_MODEL), D_FF ** -0.5),
        'ple_norm_g': gain((DEPTH, D_MODEL)),
        'w_ple_gate': nrm((DEPTH, D_MODEL, D_MODEL), D_MODEL ** -0.5),
        'w_ple_proj': nrm((DEPTH, PLE_DIM, D_MODEL), PLE_DIM ** -0.5),
    }


def reference(x_prompt, x_sample, p_prompt, p_sample, cache_k, cache_v, state_gla, page_table,
              attn_norm_g, w_in, w_gk2, b_gk, q_norm_g, k_norm_g, lam_params, gla_norm_g,
              diff_norm_g, w_out, ffn_norm_g, w_ffn_gate, w_ffn_up, w_ffn_down, ple_norm_g,
              w_ple_gate, w_ple_proj):
    B, S, _ = x_prompt.shape
    DB, T, _ = x_sample.shape
    n_pages = page_table.shape[1]
    past_len = n_pages * PAGE_SIZE
    pos_prompt = jnp.arange(S)
    pos_sample = past_len + jnp.arange(T)

    hp, hs = x_prompt, x_sample
    kp_l, vp_l, gp_l, ks_l, vs_l, gs_l = [], [], [], [], [], []
    for i in range(DEPTH):
        lam_init = lambda_init_fn(i)
        weights = (attn_norm_g[i], w_in[i], w_gk2[i], b_gk[i], q_norm_g[i], k_norm_g[i],
                   lam_params[i], gla_norm_g[i], diff_norm_g[i], w_out[i], ffn_norm_g[i],
                   w_ffn_gate[i], w_ffn_up[i], w_ffn_down[i], ple_norm_g[i], w_ple_gate[i],
                   w_ple_proj[i])
        s0 = jnp.zeros((B, GLA_HEADS, GLA_DK, GLA_DV), x_prompt.dtype)
        hp, k_new_p, v_new_p, g_new_p = trunk_layer(hp, p_prompt[i], pos_prompt, s0,
                                                     diff_attend_prompt, lam_init, *weights)
        k_past = cache_k[i, page_table].reshape(DB, past_len, DIFF_HEADS, 2, DIFF_DH)
        v_past = cache_v[i, page_table].reshape(DB, past_len, DIFF_HEADS, DIFF_DV)

        def attend_sample(q, k, v, lam, k_past=k_past, v_past=v_past):
            return diff_attend_sample(q, k, v, lam, k_past, v_past)

        hs, k_new_s, v_new_s, g_new_s = trunk_layer(hs, p_sample[i], pos_sample, state_gla[i],
                                                    attend_sample, lam_init, *weights)
        kp_l.append(k_new_p); vp_l.append(v_new_p); gp_l.append(g_new_p)
        ks_l.append(k_new_s); vs_l.append(v_new_s); gs_l.append(g_new_s)

    new_k_prompt = jnp.stack(kp_l)
    new_v_prompt = jnp.stack(vp_l)
    new_gla_prompt = jnp.stack(gp_l)
    new_k_sample = jnp.stack(ks_l)
    new_v_sample = jnp.stack(vs_l)
    new_gla_sample = jnp.stack(gs_l)
    return (hp, hs, new_k_prompt, new_v_prompt, new_gla_prompt, new_k_sample, new_v_sample, new_gla_sample)
```

```python
import functools
import math

import jax
import jax.numpy as jnp
from jax import lax
from jax.experimental import pallas as pl
from jax.experimental.pallas import tpu as pltpu

F32 = jnp.float32
BF16 = jnp.bfloat16

D_MODEL = 1024
GLA_HEADS = 4
GLA_DV = 128
GLA_DK = 64
GLA_GATE_RANK = 16
GLA_GATE_NORMALIZER = 16.0
GLA_CHUNK = 64
DIFF_HEADS = 4
DIFF_DV = 128
DIFF_DH = 64
ROPE_THETA = 10000.0
PAGE_SIZE = 128
EPS = 1e-6
GK_W = GLA_HEADS * GLA_DK
GV_W = GLA_HEADS * GLA_DV
DQ_W = DIFF_HEADS * 2 * DIFF_DH
DV_W = DIFF_HEADS * DIFF_DV
LANES = 128
LR_PAD = LANES
PROJ_W = 2 * GK_W + 2 * GV_W + 2 * DQ_W + DV_W
VMEM_LIMIT = 56 * 1024 * 1024


def _dot(a, b):
    return jnp.dot(a, b, preferred_element_type=F32)


def _dot_nt(a, b):
    return lax.dot_general(a, b, (((1,), (1,)), ((), ())), preferred_element_type=F32)


def _dot_tn(a, b):
    return lax.dot_general(a, b, (((0,), (0,)), ((), ())), preferred_element_type=F32)


def _rms(x, g):
    return x * lax.rsqrt(jnp.mean(x * x, axis=-1, keepdims=True) + EPS) * g


def _const_spec(shape):
    nd = len(shape)
    return pl.BlockSpec(shape, lambda *_: (0,) * nd, pipeline_mode=pl.Buffered(1))


def _proj_kernel(x_ref, ng_ref, w_ref, wlr_ref, wgk2_ref, bgk_ref, qg_ref, kg_ref, cos_ref,
                 sin_ref, ind_ref, gq_ref, gk_ref, gv_ref, gg_ref, la_ref, dq_ref, dk_ref, dv_ref):
    x = x_ref[...]
    xn = _rms(x, ng_ref[...]).astype(BF16)
    proj = _dot(xn, w_ref[...])
    o = 0
    gq_ref[...] = proj[:, o:o + GK_W] * (GLA_DK ** -0.5); o += GK_W
    gk_ref[...] = proj[:, o:o + GK_W]; o += GK_W
    gv_ref[...] = proj[:, o:o + GV_W]; o += GV_W
    gg_ref[...] = proj[:, o:o + GV_W]; o += GV_W
    dq = proj[:, o:o + DQ_W]; o += DQ_W
    dk = proj[:, o:o + DQ_W]; o += DQ_W
    dv_ref[...] = proj[:, o:o + DV_W]

    glr = _dot(xn, wlr_ref[...])
    z = _dot(glr.astype(BF16), wgk2_ref[...]) + bgk_ref[...]
    logsig = jnp.minimum(z, 0.0) - jnp.log(1.0 + jnp.exp(-jnp.abs(z)))
    la_ref[...] = logsig * (1.0 / GLA_GATE_NORMALIZER)

    tm = x.shape[0]
    cos = jnp.concatenate([cos_ref[...]] * (DQ_W // LANES), axis=1)
    sin = jnp.concatenate([sin_ref[...]] * (DQ_W // LANES), axis=1)
    lane = lax.broadcasted_iota(jnp.int32, (tm, DQ_W), 1)
    upper = (lane & (DIFF_DH // 2)) != 0
    ind = ind_ref[...]

    def norm_rope(y, g):
        ms = _dot((y * y).astype(BF16), ind) * (1.0 / DIFF_DH)
        yn = y * lax.rsqrt(ms + EPS) * g
        partner = jnp.where(upper, pltpu.roll(yn, DIFF_DH // 2, 1),
                            pltpu.roll(yn, DQ_W - DIFF_DH // 2, 1))
        return yn * cos + partner * sin

    dq_ref[...] = norm_rope(dq, qg_ref[...])
    dk_ref[...] = norm_rope(dk, kg_ref[...])


def _proj(x, pos_cos, pos_sin, w, *, tm, pos_period_blocks):
    n = x.shape[0]
    grid = (n // tm,)
    tok = lambda width: pl.BlockSpec((tm, width), lambda i: (i, 0))
    pos = pl.BlockSpec((tm, LANES), lambda i: (i % pos_period_blocks, 0))
    out_widths = (GK_W, GK_W, GV_W, GV_W, GK_W, DQ_W, DQ_W, DV_W)
    return pl.pallas_call(
        _proj_kernel,
        grid=grid,
        in_specs=[tok(D_MODEL), _const_spec((1, D_MODEL)), _const_spec((D_MODEL, PROJ_W)),
                  _const_spec((D_MODEL, LR_PAD)), _const_spec((LR_PAD, GK_W)),
                  _const_spec((1, GK_W)), _const_spec((1, DQ_W)), _const_spec((1, DQ_W)),
                  pos, pos, _const_spec((DQ_W, DQ_W))],
        out_specs=[tok(wd) for wd in out_widths],
        out_shape=[jax.ShapeDtypeStruct((n, wd), F32) for wd in out_widths],
        compiler_params=pltpu.CompilerParams(dimension_semantics=("arbitrary",),
                                             vmem_limit_bytes=VMEM_LIMIT),
        name="proj",
    )(x, w["attn_norm_g"], w["w_proj"], w["w_lr"], w["w_gk2"], w["b_gk"], w["q_norm_g"],
      w["k_norm_g"], pos_cos, pos_sin, w["group_ind"])


def _gla_kernel(q_ref, k_ref, v_ref, la_ref, s0_ref, o_ref, sout_ref, s_scr, *, chunk, n_chunks):
    j = pl.program_id(1)

    @pl.when(j == 0)
    def _():
        s_scr[...] = s0_ref[...]

    row = lax.broadcasted_iota(jnp.int32, (chunk, chunk), 0)
    col = lax.broadcasted_iota(jnp.int32, (chunk, chunk), 1)
    tril = row >= col
    tri = jnp.where(tril, 1.0, 0.0).astype(BF16)
    ones = jnp.ones((chunk, GLA_DV), BF16)

    def body(c, carry):
        r = pl.ds(pl.multiple_of(c * chunk, chunk), chunk)
        q = q_ref[r, :]
        k = k_ref[r, :]
        vb = v_ref[r, :].astype(BF16)
        la = la_ref[r, :]
        la_hi = la.astype(BF16)
        la_lo = (la - la_hi.astype(F32)).astype(BF16)
        b = _dot(tri, la_hi) + _dot(tri, la_lo)
        b_last = b[chunk - 1:chunk, :]
        qe = (q * jnp.exp(b)).astype(BF16)
        ke = (k * jnp.exp(-b)).astype(BF16)
        kl = (k * jnp.exp(b_last - b)).astype(BF16)
        outs = []
        for h in range(GLA_HEADS):
            ks = slice(h * GLA_DK, (h + 1) * GLA_DK)
            vs = slice(h * GLA_DV, (h + 1) * GLA_DV)
            s_h = s_scr[ks, :]
            a = jnp.where(tril, _dot_nt(qe[:, ks], ke[:, ks]), 0.0)
            outs.append(_dot(a.astype(BF16), vb[:, vs]) + _dot(qe[:, ks], s_h.astype(BF16)))
            dcol = _dot_tn(la_hi[:, ks], ones) + _dot_tn(la_lo[:, ks], ones)
            s_scr[ks, :] = jnp.exp(dcol) * s_h + _dot_tn(kl[:, ks], vb[:, vs])
        o_ref[r, :] = jnp.concatenate(outs, axis=1)
        return carry

    lax.fori_loop(0, n_chunks, body, 0)

    @pl.when(j == pl.num_programs(1) - 1)
    def _():
        sout_ref[...] = s_scr[...]


def _gla(q, k, v, la, s0, *, chunk, block):
    b, t, _ = q.shape
    grid = (b, t // block)
    tok = lambda width: pl.BlockSpec((None, block, width), lambda i, j: (i, j, 0))
    st = pl.BlockSpec((None, GK_W, GLA_DV), lambda i, j: (i, 0, 0))
    return pl.pallas_call(
        functools.partial(_gla_kernel, chunk=chunk, n_chunks=block // chunk),
        grid=grid,
        in_specs=[tok(GK_W), tok(GK_W), tok(GV_W), tok(GK_W), st],
        out_specs=[tok(GV_W), st],
        out_shape=[jax.ShapeDtypeStruct((b, t, GV_W), F32),
                   jax.ShapeDtypeStruct((b, GK_W, GLA_DV), F32)],
        scratch_shapes=[pltpu.VMEM((GK_W, GLA_DV), F32)],
        compiler_params=pltpu.CompilerParams(dimension_semantics=("arbitrary", "arbitrary"),
                                             vmem_limit_bytes=VMEM_LIMIT),
        name="gla",
    )(q, k, v, la, s0)


def _lambda(lp_ref, lam_init):
    lp = lp_ref[...]
    s1 = jnp.sum(lp[0:1, :] * lp[1:2, :], axis=1, keepdims=True)
    s2 = jnp.sum(lp[2:3, :] * lp[3:4, :], axis=1, keepdims=True)
    return jnp.exp(s1) - jnp.exp(s2) + lam_init


def _softmax_step(s, vb, m_scr, l_scr, acc_scr):
    m_old = m_scr[...]
    m_new = jnp.maximum(m_old, jnp.max(s, axis=1, keepdims=True))
    alpha = jnp.exp(m_old - m_new)
    p = jnp.exp(s - m_new)
    l_scr[...] = alpha * l_scr[...] + jnp.sum(p, axis=1, keepdims=True)
    acc_scr[...] = alpha * acc_scr[...] + _dot(p.astype(BF16), vb)
    m_scr[...] = m_new


def _attn_prompt_kernel(lp_ref, q_ref, k_ref, v_ref, o_ref, m_scr, l_scr, acc_scr, *, tq, lam_init):
    i = pl.program_id(2)
    q = q_ref[...] * (DIFF_DH ** -0.5)
    lane = lax.broadcasted_iota(jnp.int32, (tq, 2 * DIFF_DH), 1)
    first = lane < DIFF_DH
    qs = jnp.concatenate([jnp.where(first, q, 0.0), jnp.where(first, 0.0, q)], axis=0).astype(BF16)

    m_scr[...] = jnp.full(m_scr.shape, -jnp.inf, F32)
    l_scr[...] = jnp.zeros(l_scr.shape, F32)
    acc_scr[...] = jnp.zeros(acc_scr.shape, F32)

    def body(j, carry):
        r = pl.ds(pl.multiple_of(j * tq, tq), tq)
        s = _dot_nt(qs, k_ref[r, :].astype(BF16))
        _softmax_step(s, v_ref[r, :].astype(BF16), m_scr, l_scr, acc_scr)
        return carry

    lax.fori_loop(0, i, body, 0)

    r = pl.ds(pl.multiple_of(i * tq, tq), tq)
    s = _dot_nt(qs, k_ref[r, :].astype(BF16))
    qpos = lax.broadcasted_iota(jnp.int32, (2 * tq, tq), 0) % tq
    kpos = lax.broadcasted_iota(jnp.int32, (2 * tq, tq), 1)
    s = jnp.where(kpos <= qpos, s, -jnp.inf)
    _softmax_step(s, v_ref[r, :].astype(BF16), m_scr, l_scr, acc_scr)

    out = acc_scr[...] / l_scr[...]
    o_ref[...] = out[:tq, :] - _lambda(lp_ref, lam_init) * out[tq:, :]


def _attn_prompt(q, k, v, lam_params, lam_init, *, tq):
    b, s, _ = q.shape
    grid = (b, DIFF_HEADS, s // tq)
    width = 2 * DIFF_DH
    qspec = pl.BlockSpec((None, tq, width), lambda bi, h, i: (bi, i, h))
    kvspec = pl.BlockSpec((None, s, width), lambda bi, h, i: (bi, 0, h))
    return pl.pallas_call(
        functools.partial(_attn_prompt_kernel, tq=tq, lam_init=lam_init),
        grid=grid,
        in_specs=[_const_spec((4, DIFF_DH)), qspec, kvspec, kvspec],
        out_specs=qspec,
        out_shape=jax.ShapeDtypeStruct((b, s, DV_W), F32),
        scratch_shapes=[pltpu.VMEM((2 * tq, 1), F32), pltpu.VMEM((2 * tq, 1), F32),
                        pltpu.VMEM((2 * tq, DIFF_DV), F32)],
        compiler_params=pltpu.CompilerParams(
            dimension_semantics=("arbitrary", "arbitrary", "arbitrary"),
            vmem_limit_bytes=VMEM_LIMIT),
        name="attn_prompt",
    )(lam_params, q, k, v)


def _attn_sample_kernel(pt_ref, lp_ref, q_ref, kn_ref, vn_ref, *refs, pages, t_new, lam_init):
    k_refs = refs[:pages]
    v_refs = refs[pages:2 * pages]
    o_ref = refs[2 * pages]
    qs_scr, m_scr, l_scr, acc_scr = refs[2 * pages + 1:]
    p = pl.program_id(1)
    n_rows = 2 * DIFF_HEADS * t_new

    @pl.when(p == 0)
    def _():
        q = q_ref[...] * (DIFF_DH ** -0.5)
        lane = lax.broadcasted_iota(jnp.int32, (t_new, DQ_W), 1) // DIFF_DH
        qs = jnp.concatenate([jnp.where(lane == g, q, 0.0) for g in range(2 * DIFF_HEADS)], axis=0)
        qs_scr[...] = qs.astype(BF16)
        s = _dot_nt(qs.astype(BF16), kn_ref[...].astype(BF16))
        qpos = lax.broadcasted_iota(jnp.int32, (n_rows, t_new), 0) % t_new
        kpos = lax.broadcasted_iota(jnp.int32, (n_rows, t_new), 1)
        s = jnp.where(kpos <= qpos, s, -jnp.inf)
        m = jnp.max(s, axis=1, keepdims=True)
        e = jnp.exp(s - m)
        m_scr[...] = m
        l_scr[...] = jnp.sum(e, axis=1, keepdims=True)
        acc_scr[...] = _dot(e.astype(BF16), vn_ref[...].astype(BF16))

    qs = qs_scr[...]
    for g in range(pages):
        s = _dot_nt(qs, k_refs[g][...].astype(BF16))
        _softmax_step(s, v_refs[g][...].astype(BF16), m_scr, l_scr, acc_scr)

    @pl.when(p == pl.num_programs(1) - 1)
    def _():
        out = acc_scr[...] / l_scr[...]
        lam = _lambda(lp_ref, lam_init)
        heads = []
        for h in range(DIFF_HEADS):
            r1 = (2 * h) * t_new
            r2 = (2 * h + 1) * t_new
            vs = slice(h * DIFF_DV, (h + 1) * DIFF_DV)
            heads.append(out[r1:r1 + t_new, vs] - lam * out[r2:r2 + t_new, vs])
        o_ref[...] = jnp.concatenate(heads, axis=1)


def _attn_sample(q, k_new, v_new, cache_k, cache_v, page_table, lam_params, lam_init, *, pages):
    db, t_new, _ = q.shape
    n_pages = page_table.shape[1]
    grid = (db, n_pages // pages)
    n_rows = 2 * DIFF_HEADS * t_new
    tok = pl.BlockSpec((None, t_new, DQ_W), lambda b, p, pt: (b, 0, 0))

    def page_spec(g):
        return pl.BlockSpec((None, PAGE_SIZE, DQ_W), lambda b, p, pt: (pt[b, p * pages + g], 0, 0))

    grid_spec = pltpu.PrefetchScalarGridSpec(
        num_scalar_prefetch=1,
        grid=grid,
        in_specs=[pl.BlockSpec((4, DIFF_DH), lambda b, p, pt: (0, 0)), tok, tok, tok]
        + [page_spec(g) for g in range(pages)] + [page_spec(g) for g in range(pages)],
        out_specs=tok,
        scratch_shapes=[pltpu.VMEM((n_rows, DQ_W), BF16), pltpu.VMEM((n_rows, 1), F32),
                        pltpu.VMEM((n_rows, 1), F32), pltpu.VMEM((n_rows, DV_W), F32)],
    )
    return pl.pallas_call(
        functools.partial(_attn_sample_kernel, pages=pages, t_new=t_new, lam_init=lam_init),
        grid_spec=grid_spec,
        out_shape=jax.ShapeDtypeStruct((db, t_new, DV_W), F32),
        compiler_params=pltpu.CompilerParams(dimension_semantics=("arbitrary", "arbitrary"),
                                             vmem_limit_bytes=VMEM_LIMIT),
        name="attn_sample",
    )(page_table, lam_params, q, k_new, v_new, *([cache_k] * pages), *([cache_v] * pages))


def _out_ffn_kernel(x_ref, go_ref, gg_ref, do_ref, p_ref, gn_ref, dn_ref, wout_ref, fn_ref,
                    wg_ref, wu_ref, wd_ref, pn_ref, wpg_ref, wpp_ref, y_ref, *, lam_init):
    def head_norm(o, g):
        return jnp.concatenate(
            [_rms(o[:, h * LANES:(h + 1) * LANES], g) for h in range(o.shape[1] // LANES)], axis=1)

    gg = gg_ref[...]
    go = head_norm(go_ref[...], gn_ref[...]) * (gg * jax.nn.sigmoid(gg))
    do = head_norm(do_ref[...], dn_ref[...]) * (1.0 - lam_init)
    mix = jnp.concatenate([go, do], axis=1).astype(BF16)
    h = x_ref[...] + _dot(mix, wout_ref[...])
    hn = _rms(h, fn_ref[...]).astype(BF16)
    gate = _dot(hn, wg_ref[...])
    f = (gate * jax.nn.sigmoid(gate)) * _dot(hn, wu_ref[...])
    h = h + _dot(f.astype(BF16), wd_ref[...])
    pg = jax.nn.sigmoid(_dot(_rms(h, pn_ref[...]).astype(BF16), wpg_ref[...]))
    y_ref[...] = h + pg * _dot(p_ref[...].astype(BF16), wpp_ref[...])


def _out_ffn(x, go, gg, do, p, w, lam_init, *, tm):
    n = x.shape[0]
    d_ff = w["w_ffn_gate"].shape[1]
    ple = p.shape[1]
    tok = lambda width: pl.BlockSpec((tm, width), lambda i: (i, 0))
    return pl.pallas_call(
        functools.partial(_out_ffn_kernel, lam_init=lam_init),
        grid=(n // tm,),
        in_specs=[tok(D_MODEL), tok(GV_W), tok(GV_W), tok(DV_W), tok(ple),
                  _const_spec((1, GLA_DV)), _const_spec((1, DIFF_DV)),
                  _const_spec((D_MODEL, D_MODEL)), _const_spec((1, D_MODEL)),
                  _const_spec((D_MODEL, d_ff)), _const_spec((D_MODEL, d_ff)),
                  _const_spec((d_ff, D_MODEL)), _const_spec((1, D_MODEL)),
                  _const_spec((D_MODEL, D_MODEL)), _const_spec((ple, D_MODEL))],
        out_specs=tok(D_MODEL),
        out_shape=jax.ShapeDtypeStruct((n, D_MODEL), F32),
        compiler_params=pltpu.CompilerParams(dimension_semantics=("arbitrary",),
                                             vmem_limit_bytes=VMEM_LIMIT),
        name="out_ffn",
    )(x, go, gg, do, p, w["gla_norm_g"], w["diff_norm_g"], w["w_out"], w["ffn_norm_g"],
      w["w_ffn_gate"], w["w_ffn_up"], w["w_ffn_down"], w["ple_norm_g"], w["w_ple_gate"],
      w["w_ple_proj"])


def _rope_tables(pos):
    half = DIFF_DH // 2
    inv = ROPE_THETA ** (-jnp.arange(half, dtype=F32) / half)
    ang = pos.astype(F32)[:, None] * inv[None, :]
    cos, sin = jnp.cos(ang), jnp.sin(ang)
    cos = jnp.concatenate([cos, cos], axis=1)
    sin = jnp.concatenate([-sin, sin], axis=1)
    reps = LANES // DIFF_DH
    return jnp.tile(cos, (1, reps)), jnp.tile(sin, (1, reps))


def _layer_weights(i, attn_norm_g, w_in, w_gk2, b_gk, q_norm_g, k_norm_g, lam_params, gla_norm_g,
                   diff_norm_g, w_out, ffn_norm_g, w_ffn_gate, w_ffn_up, w_ffn_down, ple_norm_g,
                   w_ple_gate, w_ple_proj):
    lr0 = 2 * GK_W + 2 * GV_W
    wi = w_in[i]
    w_lr = jnp.pad(wi[:, lr0:lr0 + GLA_GATE_RANK], ((0, 0), (0, LR_PAD - GLA_GATE_RANK)))
    gid = jnp.arange(DQ_W) // DIFF_DH
    row = lambda a: a.reshape(1, -1).astype(F32)
    return {
        "attn_norm_g": row(attn_norm_g[i]),
        "w_proj": jnp.concatenate([wi[:, :lr0], wi[:, lr0 + GLA_GATE_RANK:]], axis=1).astype(BF16),
        "w_lr": w_lr.astype(BF16),
        "w_gk2": jnp.pad(w_gk2[i], ((0, LR_PAD - GLA_GATE_RANK), (0, 0))).astype(BF16),
        "b_gk": row(b_gk[i]),
        "q_norm_g": row(jnp.tile(q_norm_g[i], DQ_W // DIFF_DH)),
        "k_norm_g": row(jnp.tile(k_norm_g[i], DQ_W // DIFF_DH)),
        "group_ind": (gid[:, None] == gid[None, :]).astype(BF16),
        "lam_params": lam_params[i].astype(F32),
        "gla_norm_g": row(gla_norm_g[i]),
        "diff_norm_g": row(diff_norm_g[i]),
        "w_out": w_out[i].astype(BF16),
        "ffn_norm_g": row(ffn_norm_g[i]),
        "w_ffn_gate": w_ffn_gate[i].astype(BF16),
        "w_ffn_up": w_ffn_up[i].astype(BF16),
        "w_ffn_down": w_ffn_down[i].astype(BF16),
        "ple_norm_g": row(ple_norm_g[i]),
        "w_ple_gate": w_ple_gate[i].astype(BF16),
        "w_ple_proj": w_ple_proj[i].astype(BF16),
    }


def kernel(x_prompt, x_sample, p_prompt, p_sample, cache_k, cache_v, state_gla, page_table, attn_norm_g, w_in, w_gk2, b_gk, q_norm_g, k_norm_g, lam_params, gla_norm_g, diff_norm_g, w_out, ffn_norm_g, w_ffn_gate, w_ffn_up, w_ffn_down, ple_norm_g, w_ple_gate, w_ple_proj):
    B, S, _ = x_prompt.shape
    DB, T, _ = x_sample.shape
    depth = w_in.shape[0]
    n_pool = cache_k.shape[1]
    n_pages = page_table.shape[1]
    past_len = n_pages * PAGE_SIZE

    tm_p = 512
    cos_p, sin_p = _rope_tables(jnp.arange(S))
    cos_s, sin_s = _rope_tables(jnp.tile(past_len + jnp.arange(T), DB))
    t_pad = 16
    cache_k2 = cache_k.reshape(depth * n_pool, PAGE_SIZE, DQ_W)
    cache_v2 = cache_v.reshape(depth * n_pool, PAGE_SIZE, DV_W)

    hp = x_prompt.reshape(B * S, D_MODEL)
    hs = x_sample.reshape(DB * T, D_MODEL)
    kp_l, vp_l, gp_l, ks_l, vs_l, gs_l = [], [], [], [], [], []
    for i in range(depth):
        lam_init = 0.8 - 0.6 * math.exp(-0.3 * i)
        w = _layer_weights(i, attn_norm_g, w_in, w_gk2, b_gk, q_norm_g, k_norm_g, lam_params,
                           gla_norm_g, diff_norm_g, w_out, ffn_norm_g, w_ffn_gate, w_ffn_up,
                           w_ffn_down, ple_norm_g, w_ple_gate, w_ple_proj)

        gq, gk, gv, gg, la, dq, dk, dv = _proj(hp, cos_p, sin_p, w, tm=tm_p,
                                               pos_period_blocks=S // tm_p)
        r3 = lambda a: a.reshape(B, S, a.shape[-1])
        s0 = jnp.zeros((B, GK_W, GLA_DV), F32)
        gla_o, gla_s = _gla(r3(gq), r3(gk), r3(gv), r3(la), s0, chunk=GLA_CHUNK, block=512)
        diff_o = _attn_prompt(r3(dq), r3(dk), r3(dv), w["lam_params"], lam_init, tq=256)
        hp = _out_ffn(hp, gla_o.reshape(B * S, GV_W), gg, diff_o.reshape(B * S, DV_W),
                      p_prompt[i].reshape(B * S, -1), w, lam_init, tm=256)
        kp_l.append(dk.reshape(B, S, DIFF_HEADS, 2, DIFF_DH))
        vp_l.append(dv.reshape(B, S, DIFF_HEADS, DIFF_DV))
        gp_l.append(gla_s.reshape(B, GLA_HEADS, GLA_DK, GLA_DV))

        gq, gk, gv, gg, la, dq, dk, dv = _proj(hs, cos_s, sin_s, w, tm=DB * T, pos_period_blocks=1)
        r3 = lambda a: a.reshape(DB, T, a.shape[-1])
        padt = lambda a: jnp.pad(r3(a), ((0, 0), (0, t_pad - T), (0, 0)))
        s0 = state_gla[i].reshape(DB, GK_W, GLA_DV)
        gla_o, gla_s = _gla(padt(gq), padt(gk), padt(gv), padt(la), s0, chunk=t_pad, block=t_pad)
        gla_o = gla_o[:, :T]
        diff_o = _attn_sample(r3(dq), r3(dk), r3(dv), cache_k2, cache_v2, page_table + i * n_pool,
                              w["lam_params"], lam_init, pages=8)
        hs = _out_ffn(hs, gla_o.reshape(DB * T, GV_W), gg, diff_o.reshape(DB * T, DV_W),
                      p_sample[i].reshape(DB * T, -1), w, lam_init, tm=DB * T)
        ks_l.append(dk.reshape(DB, T, DIFF_HEADS, 2, DIFF_DH))
        vs_l.append(dv.reshape(DB, T, DIFF_HEADS, DIFF_DV))
        gs_l.append(gla_s.reshape(DB, GLA_HEADS, GLA_DK, GLA_DV))

    return (hp.reshape(B, S, D_MODEL), hs.reshape(DB, T, D_MODEL),
            jnp.stack(kp_l), jnp.stack(vp_l), jnp.stack(gp_l),
            jnp.stack(ks_l), jnp.stack(vs_l), jnp.stack(gs_l))
```

```python
import functools
import math

import jax
import jax.numpy as jnp
from jax import lax
from jax.experimental import pallas as pl
from jax.experimental.pallas import tpu as pltpu

F32 = jnp.float32
BF16 = jnp.bfloat16

D_MODEL = 1024
GLA_HEADS = 4
GLA_DV = 128
GLA_DK = 64
GLA_GATE_RANK = 16
GLA_GATE_NORMALIZER = 16.0
GLA_CHUNK = 64
DIFF_HEADS = 4
DIFF_DV = 128
DIFF_DH = 64
ROPE_THETA = 10000.0
PAGE_SIZE = 128
EPS = 1e-6
GK_W = GLA_HEADS * GLA_DK
GV_W = GLA_HEADS * GLA_DV
DQ_W = DIFF_HEADS * 2 * DIFF_DH
DV_W = DIFF_HEADS * DIFF_DV
LANES = 128
LR_PAD = LANES
PROJ_W = 2 * GK_W + 2 * GV_W + 2 * DQ_W + DV_W
VMEM_LIMIT = 56 * 1024 * 1024


def _dot(a, b):
    return jnp.dot(a, b, preferred_element_type=F32)


def _dot_nt(a, b):
    return lax.dot_general(a, b, (((1,), (1,)), ((), ())), preferred_element_type=F32)


def _dot_tn(a, b):
    return lax.dot_general(a, b, (((0,), (0,)), ((), ())), preferred_element_type=F32)


def _rms(x, g):
    return x * lax.rsqrt(jnp.mean(x * x, axis=-1, keepdims=True) + EPS) * g


def _const_spec(shape):
    nd = len(shape)
    return pl.BlockSpec(shape, lambda *_: (0,) * nd, pipeline_mode=pl.Buffered(1))


def _proj_kernel(x_ref, ng_ref, w_ref, wlr_ref, wgk2_ref, bgk_ref, qg_ref, kg_ref, cos_ref,
                 sin_ref, ind_ref, gq_ref, gk_ref, gv_ref, gg_ref, la_ref, dq_ref, dk_ref, dv_ref):
    x = x_ref[...]
    xn = _rms(x, ng_ref[...]).astype(BF16)
    proj = _dot(xn, w_ref[...])
    o = 0
    gq_ref[...] = proj[:, o:o + GK_W] * (GLA_DK ** -0.5); o += GK_W
    gk_ref[...] = proj[:, o:o + GK_W]; o += GK_W
    gv_ref[...] = proj[:, o:o + GV_W]; o += GV_W
    gg_ref[...] = proj[:, o:o + GV_W]; o += GV_W
    dq = proj[:, o:o + DQ_W]; o += DQ_W
    dk = proj[:, o:o + DQ_W]; o += DQ_W
    dv_ref[...] = proj[:, o:o + DV_W]

    glr = _dot(xn, wlr_ref[...])
    z = _dot(glr.astype(BF16), wgk2_ref[...]) + bgk_ref[...]
    logsig = jnp.minimum(z, 0.0) - jnp.log(1.0 + jnp.exp(-jnp.abs(z)))
    la_ref[...] = logsig * (1.0 / GLA_GATE_NORMALIZER)

    tm = x.shape[0]
    cos = jnp.concatenate([cos_ref[...]] * (DQ_W // LANES), axis=1)
    sin = jnp.concatenate([sin_ref[...]] * (DQ_W // LANES), axis=1)
    lane = lax.broadcasted_iota(jnp.int32, (tm, DQ_W), 1)
    upper = (lane & (DIFF_DH // 2)) != 0
    ind = ind_ref[...]

    def norm_rope(y, g):
        ms = _dot((y * y).astype(BF16), ind) * (1.0 / DIFF_DH)
        yn = y * lax.rsqrt(ms + EPS) * g
        partner = jnp.where(upper, pltpu.roll(yn, DIFF_DH // 2, 1),
                            pltpu.roll(yn, DQ_W - DIFF_DH // 2, 1))
        return yn * cos + partner * sin

    dq_ref[...] = norm_rope(dq, qg_ref[...])
    dk_ref[...] = norm_rope(dk, kg_ref[...])


def _proj(x, pos_cos, pos_sin, w, *, tm, pos_period_blocks):
    n = x.shape[0]
    grid = (n // tm,)
    tok = lambda width: pl.BlockSpec((tm, width), lambda i: (i, 0))
    pos = pl.BlockSpec((tm, LANES), lambda i: (i % pos_period_blocks, 0))
    out_widths = (GK_W, GK_W, GV_W, GV_W, GK_W, DQ_W, DQ_W, DV_W)
    return pl.pallas_call(
        _proj_kernel,
        grid=grid,
        in_specs=[tok(D_MODEL), _const_spec((1, D_MODEL)), _const_spec((D_MODEL, PROJ_W)),
                  _const_spec((D_MODEL, LR_PAD)), _const_spec((LR_PAD, GK_W)),
                  _const_spec((1, GK_W)), _const_spec((1, DQ_W)), _const_spec((1, DQ_W)),
                  pos, pos, _const_spec((DQ_W, DQ_W))],
        out_specs=[tok(wd) for wd in out_widths],
        out_shape=[jax.ShapeDtypeStruct((n, wd), F32) for wd in out_widths],
        compiler_params=pltpu.CompilerParams(dimension_semantics=("arbitrary",),
                                             vmem_limit_bytes=VMEM_LIMIT),
        name="proj",
    )(x, w["attn_norm_g"], w["w_proj"], w["w_lr"], w["w_gk2"], w["b_gk"], w["q_norm_g"],
      w["k_norm_g"], pos_cos, pos_sin, w["group_ind"])


def _gla_kernel(q_ref, k_ref, v_ref, la_ref, s0_ref, o_ref, sout_ref, s_scr, *, chunk, n_chunks):
    j = pl.program_id(1)

    @pl.when(j == 0)
    def _():
        s_scr[...] = s0_ref[...]

    row = lax.broadcasted_iota(jnp.int32, (chunk, chunk), 0)
    col = lax.broadcasted_iota(jnp.int32, (chunk, chunk), 1)
    tril = row >= col
    tri = jnp.where(tril, 1.0, 0.0).astype(BF16)
    ones = jnp.ones((chunk, GLA_DV), BF16)

    def body(c, carry):
        r = pl.ds(pl.multiple_of(c * chunk, chunk), chunk)
        q = q_ref[r, :]
        k = k_ref[r, :]
        vb = v_ref[r, :].astype(BF16)
        la = la_ref[r, :]
        la_hi = la.astype(BF16)
        la_lo = (la - la_hi.astype(F32)).astype(BF16)
        b = _dot(tri, la_hi) + _dot(tri, la_lo)
        b_last = b[chunk - 1:chunk, :]
        qe = (q * jnp.exp(b)).astype(BF16)
        ke = (k * jnp.exp(-b)).astype(BF16)
        kl = (k * jnp.exp(b_last - b)).astype(BF16)
        outs = []
        for h in range(GLA_HEADS):
            ks = slice(h * GLA_DK, (h + 1) * GLA_DK)
            vs = slice(h * GLA_DV, (h + 1) * GLA_DV)
            s_h = s_scr[ks, :]
            a = jnp.where(tril, _dot_nt(qe[:, ks], ke[:, ks]), 0.0)
            outs.append(_dot(a.astype(BF16), vb[:, vs]) + _dot(qe[:, ks], s_h.astype(BF16)))
            dcol = _dot_tn(la_hi[:, ks], ones) + _dot_tn(la_lo[:, ks], ones)
            s_scr[ks, :] = jnp.exp(dcol) * s_h + _dot_tn(kl[:, ks], vb[:, vs])
        o_ref[r, :] = jnp.concatenate(outs, axis=1)
        return carry

    lax.fori_loop(0, n_chunks, body, 0)

    @pl.when(j == pl.num_programs(1) - 1)
    def _():
        sout_ref[...] = s_scr[...]


def _gla(q, k, v, la, s0, *, chunk, block):
    b, t, _ = q.shape
    grid = (b, t // block)
    tok = lambda width: pl.BlockSpec((None, block, width), lambda i, j: (i, j, 0))
    st = pl.BlockSpec((None, GK_W, GLA_DV), lambda i, j: (i, 0, 0))
    return pl.pallas_call(
        functools.partial(_gla_kernel, chunk=chunk, n_chunks=block // chunk),
        grid=grid,
        in_specs=[tok(GK_W), tok(GK_W), tok(GV_W), tok(GK_W), st],
        out_specs=[tok(GV_W), st],
        out_shape=[jax.ShapeDtypeStruct((b, t, GV_W), F32),
                   jax.ShapeDtypeStruct((b, GK_W, GLA_DV), F32)],
        scratch_shapes=[pltpu.VMEM((GK_W, GLA_DV), F32)],
        compiler_params=pltpu.CompilerParams(dimension_semantics=("arbitrary", "arbitrary"),
                                             vmem_limit_bytes=VMEM_LIMIT),
        name="gla",
    )(q, k, v, la, s0)


def _lambda(lp_ref, lam_init):
    lp = lp_ref[...]
    s1 = jnp.sum(lp[0:1, :] * lp[1:2, :], axis=1, keepdims=True)
    s2 = jnp.sum(lp[2:3, :] * lp[3:4, :], axis=1, keepdims=True)
    return jnp.exp(s1) - jnp.exp(s2) + lam_init


_RELAYOUT_ROWS = 512
ATTN_CHAIN = 256
ATTN_SLOTS = 16
ONES_ROWS = 16


def _attn_prompt_kernel(lp_ref, q_ref, k_ref, v_ref, o_ref, kb_scr, vt_scr, qst_scr, st_scr, m_scr,
                        acc_scr, *, tq, seq, lam_init):
    i = pl.program_id(2)
    cw = ATTN_CHAIN
    per_sub = tq // cw
    n_chains = 2 * per_sub

    @pl.when(i == 0)
    def _():
        for c in range(seq // _RELAYOUT_ROWS):
            r = slice(c * _RELAYOUT_ROWS, (c + 1) * _RELAYOUT_ROWS)
            kb_scr[r, :] = k_ref[r, :].astype(BF16)
            vt_scr[:DIFF_DV, r] = v_ref[r, :].T.astype(BF16)
        vt_scr[DIFF_DV:, :] = jnp.ones((ONES_ROWS, seq), BF16)

    feat = lax.broadcasted_iota(jnp.int32, (2 * DIFF_DH, _RELAYOUT_ROWS), 0)
    first = feat < DIFF_DH
    for c in range(tq // _RELAYOUT_ROWS):
        r = slice(c * _RELAYOUT_ROWS, (c + 1) * _RELAYOUT_ROWS)
        qt = (q_ref[r, :] * (DIFF_DH ** -0.5 * math.log2(math.e))).T
        qst_scr[:, r] = jnp.where(first, qt, 0.0).astype(BF16)
        qst_scr[:, tq + c * _RELAYOUT_ROWS:tq + (c + 1) * _RELAYOUT_ROWS] = (
            jnp.where(first, 0.0, qt).astype(BF16))

    m_scr[...] = jnp.full(m_scr.shape, -jnp.inf, F32)
    acc_scr[...] = jnp.zeros(acc_scr.shape, F32)

    kpos = lax.broadcasted_iota(jnp.int32, (cw, cw), 0)
    qpos = lax.broadcasted_iota(jnp.int32, (cw, cw), 1)
    causal = kpos <= qpos

    def scores(slot, r, c):
        st_scr[:, slot * cw:(slot + 1) * cw] = _dot(kb_scr[r, :], qst_scr[:, c * cw:(c + 1) * cw])

    def softmax_pv(slot, r, c, masked):
        cols = slice(c * cw, (c + 1) * cw)
        st = st_scr[:, slot * cw:(slot + 1) * cw]
        if masked:
            st = jnp.where(causal, st, -jnp.inf)
        m_old = m_scr[:, cols]
        m_new = jnp.maximum(m_old, jnp.max(st, axis=0, keepdims=True))
        p = jnp.exp2(st - m_new).astype(BF16)
        acc_scr[:, cols] = jnp.exp2(m_old - m_new) * acc_scr[:, cols] + _dot(vt_scr[:, r], p)
        m_scr[:, cols] = m_new

    def run(items):
        assert len(items) <= ATTN_SLOTS
        for slot, (r, c, _) in enumerate(items):
            scores(slot, r, c)
        for slot, (r, c, masked) in enumerate(items):
            softmax_pv(slot, r, c, masked)

    blocks_per_trip = ATTN_SLOTS // n_chains

    def body(j, carry):
        items = []
        for u in range(blocks_per_trip):
            r = pl.ds(pl.multiple_of((j * blocks_per_trip + u) * cw, cw), cw)
            items += [(r, c, False) for c in range(n_chains)]
        run(items)
        return carry

    lax.fori_loop(0, i * per_sub // blocks_per_trip, body, 0)

    items = []
    for jj in range(per_sub):
        r = pl.ds(pl.multiple_of(i * tq + jj * cw, cw), cw)
        block = [(r, c, c % per_sub == jj) for c in range(n_chains) if c % per_sub >= jj]
        if len(items) + len(block) > ATTN_SLOTS:
            run(items)
            items = []
        items += block
    run(items)

    lam = _lambda(lp_ref, lam_init)
    for c in range(per_sub):
        c1 = slice(c * cw, (c + 1) * cw)
        c2 = slice(tq + c * cw, tq + (c + 1) * cw)
        o1 = acc_scr[:DIFF_DV, c1] / acc_scr[DIFF_DV:DIFF_DV + 1, c1]
        o2 = acc_scr[:DIFF_DV, c2] / acc_scr[DIFF_DV:DIFF_DV + 1, c2]
        o_ref[c1, :] = (o1 - lam * o2).T


def _attn_prompt(q, k, v, lam_params, lam_init, *, tq):
    b, s, _ = q.shape
    per_sub = tq // ATTN_CHAIN
    assert tq % ATTN_CHAIN == 0 and ATTN_SLOTS % (2 * per_sub) == 0
    assert per_sub % (ATTN_SLOTS // (2 * per_sub)) == 0
    grid = (b, DIFF_HEADS, s // tq)
    width = 2 * DIFF_DH
    qspec = pl.BlockSpec((None, tq, width), lambda bi, h, i: (bi, i, h))
    kvspec = pl.BlockSpec((None, s, width), lambda bi, h, i: (bi, 0, h))
    return pl.pallas_call(
        functools.partial(_attn_prompt_kernel, tq=tq, seq=s, lam_init=lam_init),
        grid=grid,
        in_specs=[_const_spec((4, DIFF_DH)), qspec, kvspec, kvspec],
        out_specs=qspec,
        out_shape=jax.ShapeDtypeStruct((b, s, DV_W), F32),
        scratch_shapes=[pltpu.VMEM((s, width), BF16), pltpu.VMEM((DIFF_DV + ONES_ROWS, s), BF16),
                        pltpu.VMEM((width, 2 * tq), BF16),
                        pltpu.VMEM((ATTN_CHAIN, ATTN_SLOTS * ATTN_CHAIN), F32),
                        pltpu.VMEM((1, 2 * tq), F32),
                        pltpu.VMEM((DIFF_DV + ONES_ROWS, 2 * tq), F32)],
        compiler_params=pltpu.CompilerParams(
            dimension_semantics=("arbitrary", "arbitrary", "arbitrary"),
            vmem_limit_bytes=VMEM_LIMIT),
        name="attn_prompt",
    )(lam_params, q, k, v)


def _attn_sample_kernel(pt_ref, lp_ref, q_ref, kn_ref, vn_ref, *refs, pages, t_new, lam_init):
    k_refs = refs[:pages]
    v_refs = refs[pages:2 * pages]
    o_ref = refs[2 * pages]
    qs_scr, m_scr, l_scr, acc_scr = refs[2 * pages + 1:]
    p = pl.program_id(1)
    n_rows = 2 * DIFF_HEADS * t_new
    head_rows = 2 * t_new

    @pl.when(p == 0)
    def _():
        q = q_ref[...] * (DIFF_DH ** -0.5)
        lane = lax.broadcasted_iota(jnp.int32, (t_new, DQ_W), 1) // DIFF_DH
        qs = jnp.concatenate([jnp.where(lane == g, q, 0.0) for g in range(2 * DIFF_HEADS)],
                             axis=0).astype(BF16)
        qs_scr[...] = qs
        s = _dot_nt(qs, kn_ref[...].astype(BF16))
        qpos = lax.broadcasted_iota(jnp.int32, (n_rows, t_new), 0) % t_new
        kpos = lax.broadcasted_iota(jnp.int32, (n_rows, t_new), 1)
        s = jnp.where(kpos <= qpos, s, -jnp.inf)
        m = jnp.max(s, axis=1, keepdims=True)
        e = jnp.exp(s - m)
        m_scr[...] = m
        l_scr[...] = jnp.sum(e, axis=1, keepdims=True)
        eb = e.astype(BF16)
        vn = vn_ref[...].astype(BF16)
        acc_scr[...] = jnp.concatenate(
            [_dot(eb[h * head_rows:(h + 1) * head_rows, :], vn[:, h * DIFF_DV:(h + 1) * DIFF_DV])
             for h in range(DIFF_HEADS)], axis=0)

    qs = qs_scr[...]
    s = jnp.concatenate([_dot(qs, k_refs[g][...].astype(BF16)) for g in range(pages)], axis=1)
    m_old = m_scr[...]
    m_new = jnp.maximum(m_old, jnp.max(s, axis=1, keepdims=True))
    alpha = jnp.exp(m_old - m_new)
    e = jnp.exp(s - m_new)
    l_scr[...] = alpha * l_scr[...] + jnp.sum(e, axis=1, keepdims=True)
    eb = e.astype(BF16)
    pv = []
    for h in range(DIFF_HEADS):
        rows = slice(h * head_rows, (h + 1) * head_rows)
        acc_h = None
        for g in range(pages):
            v_h = v_refs[g][pl.ds(h, PAGE_SIZE, stride=DIFF_HEADS), :].astype(BF16)
            t = _dot(eb[rows, g * PAGE_SIZE:(g + 1) * PAGE_SIZE], v_h)
            acc_h = t if acc_h is None else acc_h + t
        pv.append(acc_h)
    acc_scr[...] = alpha * acc_scr[...] + jnp.concatenate(pv, axis=0)
    m_scr[...] = m_new

    @pl.when(p == pl.num_programs(1) - 1)
    def _():
        out = acc_scr[...] / l_scr[...]
        lam = _lambda(lp_ref, lam_init)
        heads = []
        for h in range(DIFF_HEADS):
            r1 = h * head_rows
            r2 = r1 + t_new
            heads.append(out[r1:r1 + t_new, :] - lam * out[r2:r2 + t_new, :])
        o_ref[...] = jnp.concatenate(heads, axis=1)


def _attn_sample(q, k_new, v_new, cache_kt, cache_v, page_table, lam_params, lam_init, *, pages):
    db, t_new, _ = q.shape
    n_pages = page_table.shape[1]
    grid = (db, n_pages // pages)
    n_rows = 2 * DIFF_HEADS * t_new
    tok = pl.BlockSpec((None, t_new, DQ_W), lambda b, p, pt: (b, 0, 0))

    def page_spec(g):
        return pl.BlockSpec((None, DQ_W, PAGE_SIZE), lambda b, p, pt: (pt[b, p * pages + g], 0, 0))

    grid_spec = pltpu.PrefetchScalarGridSpec(
        num_scalar_prefetch=1,
        grid=grid,
        in_specs=[pl.BlockSpec((4, DIFF_DH), lambda b, p, pt: (0, 0)), tok, tok, tok]
        + [page_spec(g) for g in range(pages)] + [page_spec(g) for g in range(pages)],
        out_specs=tok,
        scratch_shapes=[pltpu.VMEM((n_rows, DQ_W), BF16), pltpu.VMEM((n_rows, 1), F32),
                        pltpu.VMEM((n_rows, 1), F32), pltpu.VMEM((n_rows, DIFF_DV), F32)],
    )
    return pl.pallas_call(
        functools.partial(_attn_sample_kernel, pages=pages, t_new=t_new, lam_init=lam_init),
        grid_spec=grid_spec,
        out_shape=jax.ShapeDtypeStruct((db, t_new, DV_W), F32),
        compiler_params=pltpu.CompilerParams(dimension_semantics=("arbitrary", "arbitrary"),
                                             vmem_limit_bytes=VMEM_LIMIT),
        name="attn_sample",
    )(page_table, lam_params, q, k_new, v_new, *([cache_kt] * pages), *([cache_v] * pages))


def _out_ffn_kernel(x_ref, go_ref, gg_ref, do_ref, p_ref, gn_ref, dn_ref, wout_ref, fn_ref,
                    wg_ref, wu_ref, wd_ref, pn_ref, wpg_ref, wpp_ref, y_ref, *, lam_init):
    def head_norm(o, g):
        return jnp.concatenate(
            [_rms(o[:, h * LANES:(h + 1) * LANES], g) for h in range(o.shape[1] // LANES)], axis=1)

    gg = gg_ref[...]
    go = head_norm(go_ref[...], gn_ref[...]) * (gg * jax.nn.sigmoid(gg))
    do = head_norm(do_ref[...], dn_ref[...]) * (1.0 - lam_init)
    mix = jnp.concatenate([go, do], axis=1).astype(BF16)
    h = x_ref[...] + _dot(mix, wout_ref[...])
    hn = _rms(h, fn_ref[...]).astype(BF16)
    gate = _dot(hn, wg_ref[...])
    f = (gate * jax.nn.sigmoid(gate)) * _dot(hn, wu_ref[...])
    h = h + _dot(f.astype(BF16), wd_ref[...])
    pg = jax.nn.sigmoid(_dot(_rms(h, pn_ref[...]).astype(BF16), wpg_ref[...]))
    y_ref[...] = h + pg * _dot(p_ref[...].astype(BF16), wpp_ref[...])


def _out_ffn(x, go, gg, do, p, w, lam_init, *, tm):
    n = x.shape[0]
    d_ff = w["w_ffn_gate"].shape[1]
    ple = p.shape[1]
    tok = lambda width: pl.BlockSpec((tm, width), lambda i: (i, 0))
    return pl.pallas_call(
        functools.partial(_out_ffn_kernel, lam_init=lam_init),
        grid=(n // tm,),
        in_specs=[tok(D_MODEL), tok(GV_W), tok(GV_W), tok(DV_W), tok(ple),
                  _const_spec((1, GLA_DV)), _const_spec((1, DIFF_DV)),
                  _const_spec((D_MODEL, D_MODEL)), _const_spec((1, D_MODEL)),
                  _const_spec((D_MODEL, d_ff)), _const_spec((D_MODEL, d_ff)),
                  _const_spec((d_ff, D_MODEL)), _const_spec((1, D_MODEL)),
                  _const_spec((D_MODEL, D_MODEL)), _const_spec((ple, D_MODEL))],
        out_specs=tok(D_MODEL),
        out_shape=jax.ShapeDtypeStruct((n, D_MODEL), F32),
        compiler_params=pltpu.CompilerParams(dimension_semantics=("arbitrary",),
                                             vmem_limit_bytes=VMEM_LIMIT),
        name="out_ffn",
    )(x, go, gg, do, p, w["gla_norm_g"], w["diff_norm_g"], w["w_out"], w["ffn_norm_g"],
      w["w_ffn_gate"], w["w_ffn_up"], w["w_ffn_down"], w["ple_norm_g"], w["w_ple_gate"],
      w["w_ple_proj"])


def _rope_tables(pos):
    half = DIFF_DH // 2
    inv = ROPE_THETA ** (-jnp.arange(half, dtype=F32) / half)
    ang = pos.astype(F32)[:, None] * inv[None, :]
    cos, sin = jnp.cos(ang), jnp.sin(ang)
    cos = jnp.concatenate([cos, cos], axis=1)
    sin = jnp.concatenate([-sin, sin], axis=1)
    reps = LANES // DIFF_DH
    return jnp.tile(cos, (1, reps)), jnp.tile(sin, (1, reps))


def _layer_weights(i, attn_norm_g, w_in, w_gk2, b_gk, q_norm_g, k_norm_g, lam_params, gla_norm_g,
                   diff_norm_g, w_out, ffn_norm_g, w_ffn_gate, w_ffn_up, w_ffn_down, ple_norm_g,
                   w_ple_gate, w_ple_proj):
    lr0 = 2 * GK_W + 2 * GV_W
    wi = w_in[i]
    w_lr = jnp.pad(wi[:, lr0:lr0 + GLA_GATE_RANK], ((0, 0), (0, LR_PAD - GLA_GATE_RANK)))
    gid = jnp.arange(DQ_W) // DIFF_DH
    row = lambda a: a.reshape(1, -1).astype(F32)
    return {
        "attn_norm_g": row(attn_norm_g[i]),
        "w_proj": jnp.concatenate([wi[:, :lr0], wi[:, lr0 + GLA_GATE_RANK:]], axis=1).astype(BF16),
        "w_lr": w_lr.astype(BF16),
        "w_gk2": jnp.pad(w_gk2[i], ((0, LR_PAD - GLA_GATE_RANK), (0, 0))).astype(BF16),
        "b_gk": row(b_gk[i]),
        "q_norm_g": row(jnp.tile(q_norm_g[i], DQ_W // DIFF_DH)),
        "k_norm_g": row(jnp.tile(k_norm_g[i], DQ_W // DIFF_DH)),
        "group_ind": (gid[:, None] == gid[None, :]).astype(BF16),
        "lam_params": lam_params[i].astype(F32),
        "gla_norm_g": row(gla_norm_g[i]),
        "diff_norm_g": row(diff_norm_g[i]),
        "w_out": w_out[i].astype(BF16),
        "ffn_norm_g": row(ffn_norm_g[i]),
        "w_ffn_gate": w_ffn_gate[i].astype(BF16),
        "w_ffn_up": w_ffn_up[i].astype(BF16),
        "w_ffn_down": w_ffn_down[i].astype(BF16),
        "ple_norm_g": row(ple_norm_g[i]),
        "w_ple_gate": w_ple_gate[i].astype(BF16),
        "w_ple_proj": w_ple_proj[i].astype(BF16),
    }


def kernel(x_prompt, x_sample, p_prompt, p_sample, cache_k, cache_v, state_gla, page_table, attn_norm_g, w_in, w_gk2, b_gk, q_norm_g, k_norm_g, lam_params, gla_norm_g, diff_norm_g, w_out, ffn_norm_g, w_ffn_gate, w_ffn_up, w_ffn_down, ple_norm_g, w_ple_gate, w_ple_proj):
    B, S, _ = x_prompt.shape
    DB, T, _ = x_sample.shape
    depth = w_in.shape[0]
    n_pool = cache_k.shape[1]
    n_pages = page_table.shape[1]
    past_len = n_pages * PAGE_SIZE

    tm_p = 512
    cos_p, sin_p = _rope_tables(jnp.arange(S))
    cos_s, sin_s = _rope_tables(jnp.tile(past_len + jnp.arange(T), DB))
    t_pad = 16
    cache_k2 = jnp.transpose(cache_k, (0, 1, 3, 4, 5, 2)).reshape(depth * n_pool, DQ_W, PAGE_SIZE)
    cache_v2 = cache_v.reshape(depth * n_pool, PAGE_SIZE * DIFF_HEADS, DIFF_DV)

    hp = x_prompt.reshape(B * S, D_MODEL)
    hs = x_sample.reshape(DB * T, D_MODEL)
    kp_l, vp_l, gp_l, ks_l, vs_l, gs_l = [], [], [], [], [], []
    for i in range(depth):
        lam_init = 0.8 - 0.6 * math.exp(-0.3 * i)
        w = _layer_weights(i, attn_norm_g, w_in, w_gk2, b_gk, q_norm_g, k_norm_g, lam_params,
                           gla_norm_g, diff_norm_g, w_out, ffn_norm_g, w_ffn_gate, w_ffn_up,
                           w_ffn_down, ple_norm_g, w_ple_gate, w_ple_proj)

        gq, gk, gv, gg, la, dq, dk, dv = _proj(hp, cos_p, sin_p, w, tm=tm_p,
                                               pos_period_blocks=S // tm_p)
        r3 = lambda a: a.reshape(B, S, a.shape[-1])
        s0 = jnp.zeros((B, GK_W, GLA_DV), F32)
        gla_o, gla_s = _gla(r3(gq), r3(gk), r3(gv), r3(la), s0, chunk=GLA_CHUNK, block=512)
        diff_o = _attn_prompt(r3(dq), r3(dk), r3(dv), w["lam_params"], lam_init, tq=1024)
        hp = _out_ffn(hp, gla_o.reshape(B * S, GV_W), gg, diff_o.reshape(B * S, DV_W),
                      p_prompt[i].reshape(B * S, -1), w, lam_init, tm=256)
        kp_l.append(dk.reshape(B, S, DIFF_HEADS, 2, DIFF_DH))
        vp_l.append(dv.reshape(B, S, DIFF_HEADS, DIFF_DV))
        gp_l.append(gla_s.reshape(B, GLA_HEADS, GLA_DK, GLA_DV))

        gq, gk, gv, gg, la, dq, dk, dv = _proj(hs, cos_s, sin_s, w, tm=DB * T, pos_period_blocks=1)
        r3 = lambda a: a.reshape(DB, T, a.shape[-1])
        padt = lambda a: jnp.pad(r3(a), ((0, 0), (0, t_pad - T), (0, 0)))
        s0 = state_gla[i].reshape(DB, GK_W, GLA_DV)
        gla_o, gla_s = _gla(padt(gq), padt(gk), padt(gv), padt(la), s0, chunk=t_pad, block=t_pad)
        gla_o = gla_o[:, :T]
        diff_o = _attn_sample(r3(dq), r3(dk), r3(dv), cache_k2, cache_v2, page_table + i * n_pool,
                              w["lam_params"], lam_init, pages=8)
        hs = _out_ffn(hs, gla_o.reshape(DB * T, GV_W), gg, diff_o.reshape(DB * T, DV_W),
                      p_sample[i].reshape(DB * T, -1), w, lam_init, tm=DB * T)
        ks_l.append(dk.reshape(DB, T, DIFF_HEADS, 2, DIFF_DH))
        vs_l.append(dv.reshape(DB, T, DIFF_HEADS, DIFF_DV))
        gs_l.append(gla_s.reshape(DB, GLA_HEADS, GLA_DK, GLA_DV))

    return (hp.reshape(B, S, D_MODEL), hs.reshape(DB, T, D_MODEL),
            jnp.stack(kp_l), jnp.stack(vp_l), jnp.stack(gp_l),
            jnp.stack(ks_l), jnp.stack(vs_l), jnp.stack(gs_l))
```

```python
import functools
import math

import jax
import jax.numpy as jnp
import numpy as np
from jax import lax
from jax.experimental import pallas as pl
from jax.experimental.pallas import tpu as pltpu

F32 = jnp.float32
BF16 = jnp.bfloat16

D_MODEL = 1024
GLA_HEADS = 4
GLA_DV = 128
GLA_DK = 64
GLA_GATE_RANK = 16
GLA_GATE_NORMALIZER = 16.0
GLA_CHUNK = 64
DIFF_HEADS = 4
DIFF_DV = 128
DIFF_DH = 64
ROPE_THETA = 10000.0
PAGE_SIZE = 128
EPS = 1e-6
GK_W = GLA_HEADS * GLA_DK
GV_W = GLA_HEADS * GLA_DV
DQ_W = DIFF_HEADS * 2 * DIFF_DH
DV_W = DIFF_HEADS * DIFF_DV
LANES = 128
LR_PAD = LANES
PROJ_W = 2 * GK_W + 2 * GV_W + 2 * DQ_W + DV_W
VMEM_LIMIT = 56 * 1024 * 1024


def _dot(a, b):
    return jnp.dot(a, b, preferred_element_type=F32)


def _dot_nt(a, b):
    return lax.dot_general(a, b, (((1,), (1,)), ((), ())), preferred_element_type=F32)


def _dot_tn(a, b):
    return lax.dot_general(a, b, (((0,), (0,)), ((), ())), preferred_element_type=F32)


def _rms(x, g):
    return x * lax.rsqrt(jnp.mean(x * x, axis=-1, keepdims=True) + EPS) * g


def _const_spec(shape):
    nd = len(shape)
    return pl.BlockSpec(shape, lambda *_: (0,) * nd, pipeline_mode=pl.Buffered(1))


def _proj_kernel(x_ref, ng_ref, w_ref, wlr_ref, wgk2_ref, bgk_ref, qg_ref, kg_ref, cos_ref,
                 sin_ref, ind_ref, gq_ref, gk_ref, gv_ref, gg_ref, la_ref, dq_ref, dk_ref, dv_ref,
                 dkt_ref=None):
    x = x_ref[...]
    xn = _rms(x, ng_ref[...]).astype(BF16)
    proj = _dot(xn, w_ref[...])
    o = 0
    gq_ref[...] = proj[:, o:o + GK_W] * (GLA_DK ** -0.5); o += GK_W
    gk_ref[...] = proj[:, o:o + GK_W]; o += GK_W
    gv_ref[...] = proj[:, o:o + GV_W]; o += GV_W
    gg_ref[...] = proj[:, o:o + GV_W]; o += GV_W
    dq = proj[:, o:o + DQ_W]; o += DQ_W
    dk = proj[:, o:o + DQ_W]; o += DQ_W
    dv_ref[...] = proj[:, o:o + DV_W]

    glr = _dot(xn, wlr_ref[...])
    z = _dot(glr.astype(BF16), wgk2_ref[...]) + bgk_ref[...]
    logsig = jnp.minimum(z, 0.0) - jnp.log(1.0 + jnp.exp(-jnp.abs(z)))
    la_ref[...] = logsig * (1.0 / GLA_GATE_NORMALIZER)

    tm = x.shape[0]
    cos = jnp.concatenate([cos_ref[...]] * (DQ_W // LANES), axis=1)
    sin = jnp.concatenate([sin_ref[...]] * (DQ_W // LANES), axis=1)
    lane = lax.broadcasted_iota(jnp.int32, (tm, DQ_W), 1)
    upper = (lane & (DIFF_DH // 2)) != 0
    ind = ind_ref[...]

    def norm_rope(y, g):
        ms = _dot((y * y).astype(BF16), ind) * (1.0 / DIFF_DH)
        yn = y * lax.rsqrt(ms + EPS) * g
        partner = jnp.where(upper, pltpu.roll(yn, DIFF_DH // 2, 1),
                            pltpu.roll(yn, DQ_W - DIFF_DH // 2, 1))
        return yn * cos + partner * sin

    dq_ref[...] = norm_rope(dq, qg_ref[...])
    dk = norm_rope(dk, kg_ref[...])
    dk_ref[...] = dk
    if dkt_ref is not None:
        dkt_ref[...] = dk.T


def _proj(x, pos_cos, pos_sin, w, *, tm, pos_period_blocks, seq=None):
    n = x.shape[0]
    grid = (n // tm,)
    tok = lambda width: pl.BlockSpec((tm, width), lambda i: (i, 0))
    pos = pl.BlockSpec((tm, LANES), lambda i: (i % pos_period_blocks, 0))
    out_widths = (GK_W, GK_W, GV_W, GV_W, GK_W, DQ_W, DQ_W, DV_W)
    out_specs = [tok(wd) for wd in out_widths]
    out_shape = [jax.ShapeDtypeStruct((n, wd), F32) for wd in out_widths]
    if seq is not None:
        per_seq = seq // tm
        out_specs.append(pl.BlockSpec((None, DQ_W, tm), lambda i: (i // per_seq, 0, i % per_seq)))
        out_shape.append(jax.ShapeDtypeStruct((n // seq, DQ_W, seq), F32))
    return pl.pallas_call(
        _proj_kernel,
        grid=grid,
        in_specs=[tok(D_MODEL), _const_spec((1, D_MODEL)), _const_spec((D_MODEL, PROJ_W)),
                  _const_spec((D_MODEL, LR_PAD)), _const_spec((LR_PAD, GK_W)),
                  _const_spec((1, GK_W)), _const_spec((1, DQ_W)), _const_spec((1, DQ_W)),
                  pos, pos, _const_spec((DQ_W, DQ_W))],
        out_specs=out_specs,
        out_shape=out_shape,
        compiler_params=pltpu.CompilerParams(dimension_semantics=("arbitrary",),
                                             vmem_limit_bytes=VMEM_LIMIT),
        name="proj",
    )(x, w["attn_norm_g"], w["w_proj"], w["w_lr"], w["w_gk2"], w["b_gk"], w["q_norm_g"],
      w["k_norm_g"], pos_cos, pos_sin, w["group_ind"])


GLA_GROUP = 128
GLA_LEVELS = GLA_CHUNK.bit_length() - 1


def _gla_constants():
    n = GLA_GROUP
    t = np.arange(n)[:, None]
    u = np.arange(n)[None, :]
    mats = [((t // GLA_CHUNK == u // GLA_CHUNK) & (u <= t)).astype(np.float32)]
    masks = []
    for lvl in range(GLA_LEVELS):
        h = 1 << lvl
        ref = (t // (2 * h)) * (2 * h) + h - 1
        upper = (t & h) != 0
        mats.append(np.where(upper & (u > ref) & (u <= t), 1.0, 0.0)
                    - np.where(~upper & (u > t) & (u <= ref), 1.0, 0.0))
        masks.append(((t // (2 * h) == u // (2 * h)) & upper & ((u & h) == 0)).astype(np.float32))
    head = np.arange(GK_W)[:, None] // GLA_DK == np.arange(GV_W)[None, :] // GLA_DV
    return (jnp.asarray(np.concatenate(mats, 0), BF16), jnp.asarray(np.concatenate(masks, 0), BF16),
            jnp.asarray(head.astype(np.float32), BF16))


def _gla_kernel(q_ref, k_ref, v_ref, la_ref, s0_ref, mexp_ref, mask_ref, expand_ref, o_ref,
                sout_ref, s_scr, ex_scr, a_scr, u_scr, d_scr, sb_scr, *, n_groups):
    j = pl.program_id(1)
    n = GLA_GROUP
    pair_w = 2 * GLA_DK
    n_pairs = GLA_HEADS // 2
    per_group = n // GLA_CHUNK

    @pl.when(j == 0)
    def _():
        s_scr[...] = s0_ref[...]

    def head_only(x, h):
        lane = lax.broadcasted_iota(jnp.int32, x.shape, 1)
        keep = lane < GLA_DK if h % 2 == 0 else lane >= GLA_DK
        return jnp.where(keep, x, 0.0)

    grp = lambda g: slice(g * n, (g + 1) * n)
    chunk_rows = lambda c: slice(c * GLA_CHUNK, (c + 1) * GLA_CHUNK)

    def chunk_b(c):
        lo = (c % per_group) * GLA_CHUNK
        return ex_scr[c // per_group, lo:lo + GLA_CHUNK, :]

    for g in range(n_groups):
        la = la_ref[grp(g), :]
        la_hi = la.astype(BF16)
        la_lo = (la - la_hi.astype(F32)).astype(BF16)
        ex_scr[g, :n, :] = _dot(mexp_ref[:n, :], la_hi) + _dot(mexp_ref[:n, :], la_lo)
        ex_scr[g, n:, :] = _dot(mexp_ref[n:, :], la_hi)

    for g in range(n_groups):
        q = q_ref[grp(g), :]
        k = k_ref[grp(g), :]
        a = [None] * GLA_HEADS
        for lvl in range(GLA_LEVELS):
            d = ex_scr[g, (lvl + 1) * n:(lvl + 2) * n, :]
            ql = q * jnp.exp(jnp.minimum(d, 0.0))
            kl = (k * jnp.exp(jnp.minimum(-d, 0.0))).astype(BF16)
            msk = mask_ref[lvl * n:(lvl + 1) * n, :] > 0
            for h in range(GLA_HEADS):
                tile = slice((h // 2) * pair_w, (h // 2 + 1) * pair_w)
                t = _dot_nt(head_only(ql[:, tile], h).astype(BF16), kl[:, tile])
                t = jnp.where(msk, t, 0.0)
                a[h] = t if a[h] is None else a[h] + t
        for h in range(GLA_HEADS):
            a_scr[g, h] = a[h].astype(BF16)

    for g in range(n_groups):
        v = v_ref[grp(g), :]
        vb = v.astype(BF16)
        o = _dot((q_ref[grp(g), :] * k_ref[grp(g), :]).astype(BF16), expand_ref[...]) * v
        o_ref[grp(g), :] = o + jnp.concatenate(
            [_dot(a_scr[g, h], vb[:, h * GLA_DV:(h + 1) * GLA_DV]) for h in range(GLA_HEADS)],
            axis=1)

    for c in range(n_groups * per_group):
        r = chunk_rows(c)
        bc = chunk_b(c)
        b_last = bc[GLA_CHUNK - 1:GLA_CHUNK, :]
        ke = (k_ref[r, :] * jnp.exp(b_last - bc)).astype(BF16)
        vb = v_ref[r, :].astype(BF16)
        decay = jnp.exp(jnp.broadcast_to(b_last, (GLA_DV, GK_W))).T
        for pair in range(n_pairs):
            tile = slice(pair * pair_w, (pair + 1) * pair_w)
            upd = []
            for h in (2 * pair, 2 * pair + 1):
                kv = _dot_tn(ke[:, tile], vb[:, h * GLA_DV:(h + 1) * GLA_DV])
                upd.append(kv[(h % 2) * GLA_DK:(h % 2 + 1) * GLA_DK])
            u_scr[c, pair] = jnp.concatenate(upd, axis=0)
            d_scr[c, pair] = decay[tile, :]

    for c in range(n_groups * per_group):
        for pair in range(n_pairs):
            tile = slice(pair * pair_w, (pair + 1) * pair_w)
            s_pair = s_scr[tile, :]
            sb_scr[c, pair] = s_pair.astype(BF16)
            s_scr[tile, :] = d_scr[c, pair] * s_pair + u_scr[c, pair]

    for c in range(n_groups * per_group):
        r = chunk_rows(c)
        qe = q_ref[r, :] * jnp.exp(chunk_b(c))
        heads = []
        for h in range(GLA_HEADS):
            tile = slice((h // 2) * pair_w, (h // 2 + 1) * pair_w)
            heads.append(_dot(head_only(qe[:, tile], h).astype(BF16), sb_scr[c, h // 2]))
        o_ref[r, :] += jnp.concatenate(heads, axis=1)

    @pl.when(j == pl.num_programs(1) - 1)
    def _():
        sout_ref[...] = s_scr[...]


def _gla(q, k, v, la, s0, *, block):
    b, t, _ = q.shape
    assert t % block == 0 and block % GLA_GROUP == 0
    mexp, masks, expand = _gla_constants()
    n_groups = block // GLA_GROUP
    n_chunks = block // GLA_CHUNK
    n_pairs = GLA_HEADS // 2
    grid = (b, t // block)
    tok = lambda width: pl.BlockSpec((None, block, width), lambda i, j: (i, j, 0))
    st = pl.BlockSpec((None, GK_W, GLA_DV), lambda i, j: (i, 0, 0))
    return pl.pallas_call(
        functools.partial(_gla_kernel, n_groups=n_groups),
        grid=grid,
        in_specs=[tok(GK_W), tok(GK_W), tok(GV_W), tok(GK_W), st, _const_spec(mexp.shape),
                  _const_spec(masks.shape), _const_spec(expand.shape)],
        out_specs=[tok(GV_W), st],
        out_shape=[jax.ShapeDtypeStruct((b, t, GV_W), F32),
                   jax.ShapeDtypeStruct((b, GK_W, GLA_DV), F32)],
        scratch_shapes=[pltpu.VMEM((GK_W, GLA_DV), F32),
                        pltpu.VMEM((n_groups, (GLA_LEVELS + 1) * GLA_GROUP, GK_W), F32),
                        pltpu.VMEM((n_groups, GLA_HEADS, GLA_GROUP, GLA_GROUP), BF16),
                        pltpu.VMEM((n_chunks, n_pairs, 2 * GLA_DK, GLA_DV), F32),
                        pltpu.VMEM((n_chunks, n_pairs, 2 * GLA_DK, GLA_DV), F32),
                        pltpu.VMEM((n_chunks, n_pairs, 2 * GLA_DK, GLA_DV), BF16)],
        compiler_params=pltpu.CompilerParams(dimension_semantics=("arbitrary", "arbitrary"),
                                             vmem_limit_bytes=VMEM_LIMIT),
        name="gla",
    )(q, k, v, la, s0, mexp, masks, expand)


def _lambda(lp_ref, lam_init):
    lp = lp_ref[...]
    s1 = jnp.sum(lp[0:1, :] * lp[1:2, :], axis=1, keepdims=True)
    s2 = jnp.sum(lp[2:3, :] * lp[3:4, :], axis=1, keepdims=True)
    return jnp.exp(s1) - jnp.exp(s2) + lam_init


_RELAYOUT_ROWS = 512
ATTN_CHAIN = 256
ATTN_SLOTS = 16
ONES_ROWS = 16


def _attn_prompt_kernel(lp_ref, q_ref, k_ref, v_ref, o_ref, kb_scr, vt_scr, qst_scr, st_scr, m_scr,
                        acc_scr, *, tq, seq, lam_init):
    i = pl.program_id(2)
    cw = ATTN_CHAIN
    per_sub = tq // cw
    n_chains = 2 * per_sub

    @pl.when(i == 0)
    def _():
        for c in range(seq // _RELAYOUT_ROWS):
            r = slice(c * _RELAYOUT_ROWS, (c + 1) * _RELAYOUT_ROWS)
            kb_scr[r, :] = k_ref[r, :].astype(BF16)
            vt_scr[:DIFF_DV, r] = v_ref[r, :].T.astype(BF16)
        vt_scr[DIFF_DV:, :] = jnp.ones((ONES_ROWS, seq), BF16)

    feat = lax.broadcasted_iota(jnp.int32, (2 * DIFF_DH, _RELAYOUT_ROWS), 0)
    first = feat < DIFF_DH
    for c in range(tq // _RELAYOUT_ROWS):
        r = slice(c * _RELAYOUT_ROWS, (c + 1) * _RELAYOUT_ROWS)
        qt = (q_ref[r, :] * (DIFF_DH ** -0.5 * math.log2(math.e))).T
        qst_scr[:, r] = jnp.where(first, qt, 0.0).astype(BF16)
        qst_scr[:, tq + c * _RELAYOUT_ROWS:tq + (c + 1) * _RELAYOUT_ROWS] = (
            jnp.where(first, 0.0, qt).astype(BF16))

    m_scr[...] = jnp.full(m_scr.shape, -jnp.inf, F32)
    acc_scr[...] = jnp.zeros(acc_scr.shape, F32)

    kpos = lax.broadcasted_iota(jnp.int32, (cw, cw), 0)
    qpos = lax.broadcasted_iota(jnp.int32, (cw, cw), 1)
    causal = kpos <= qpos

    def scores(slot, r, c):
        st_scr[:, slot * cw:(slot + 1) * cw] = _dot(kb_scr[r, :], qst_scr[:, c * cw:(c + 1) * cw])

    def softmax_pv(slot, r, c, masked):
        cols = slice(c * cw, (c + 1) * cw)
        st = st_scr[:, slot * cw:(slot + 1) * cw]
        if masked:
            st = jnp.where(causal, st, -jnp.inf)
        m_old = m_scr[:, cols]
        m_new = jnp.maximum(m_old, jnp.max(st, axis=0, keepdims=True))
        p = jnp.exp2(st - m_new).astype(BF16)
        acc_scr[:, cols] = jnp.exp2(m_old - m_new) * acc_scr[:, cols] + _dot(vt_scr[:, r], p)
        m_scr[:, cols] = m_new

    def run(items):
        assert len(items) <= ATTN_SLOTS
        for slot, (r, c, _) in enumerate(items):
            scores(slot, r, c)
        for slot, (r, c, masked) in enumerate(items):
            softmax_pv(slot, r, c, masked)

    blocks_per_trip = ATTN_SLOTS // n_chains

    def body(j, carry):
        items = []
        for u in range(blocks_per_trip):
            r = pl.ds(pl.multiple_of((j * blocks_per_trip + u) * cw, cw), cw)
            items += [(r, c, False) for c in range(n_chains)]
        run(items)
        return carry

    lax.fori_loop(0, i * per_sub // blocks_per_trip, body, 0)

    items = []
    for jj in range(per_sub):
        r = pl.ds(pl.multiple_of(i * tq + jj * cw, cw), cw)
        block = [(r, c, c % per_sub == jj) for c in range(n_chains) if c % per_sub >= jj]
        if len(items) + len(block) > ATTN_SLOTS:
            run(items)
            items = []
        items += block
    run(items)

    lam = _lambda(lp_ref, lam_init)
    for c in range(per_sub):
        c1 = slice(c * cw, (c + 1) * cw)
        c2 = slice(tq + c * cw, tq + (c + 1) * cw)
        o1 = acc_scr[:DIFF_DV, c1] / acc_scr[DIFF_DV:DIFF_DV + 1, c1]
        o2 = acc_scr[:DIFF_DV, c2] / acc_scr[DIFF_DV:DIFF_DV + 1, c2]
        o_ref[c1, :] = (o1 - lam * o2).T


def _attn_prompt(q, k, v, lam_params, lam_init, *, tq):
    b, s, _ = q.shape
    per_sub = tq // ATTN_CHAIN
    assert tq % ATTN_CHAIN == 0 and ATTN_SLOTS % (2 * per_sub) == 0
    assert per_sub % (ATTN_SLOTS // (2 * per_sub)) == 0
    grid = (b, DIFF_HEADS, s // tq)
    width = 2 * DIFF_DH
    qspec = pl.BlockSpec((None, tq, width), lambda bi, h, i: (bi, i, h))
    kvspec = pl.BlockSpec((None, s, width), lambda bi, h, i: (bi, 0, h))
    return pl.pallas_call(
        functools.partial(_attn_prompt_kernel, tq=tq, seq=s, lam_init=lam_init),
        grid=grid,
        in_specs=[_const_spec((4, DIFF_DH)), qspec, kvspec, kvspec],
        out_specs=qspec,
        out_shape=jax.ShapeDtypeStruct((b, s, DV_W), F32),
        scratch_shapes=[pltpu.VMEM((s, width), BF16), pltpu.VMEM((DIFF_DV + ONES_ROWS, s), BF16),
                        pltpu.VMEM((width, 2 * tq), BF16),
                        pltpu.VMEM((ATTN_CHAIN, ATTN_SLOTS * ATTN_CHAIN), F32),
                        pltpu.VMEM((1, 2 * tq), F32),
                        pltpu.VMEM((DIFF_DV + ONES_ROWS, 2 * tq), F32)],
        compiler_params=pltpu.CompilerParams(
            dimension_semantics=("arbitrary", "arbitrary", "arbitrary"),
            vmem_limit_bytes=VMEM_LIMIT),
        name="attn_prompt",
    )(lam_params, q, k, v)


def _attn_sample_kernel(pt_ref, lp_ref, q_ref, kn_ref, vn_ref, *refs, pages, t_new, lam_init):
    k_refs = refs[:pages]
    v_refs = refs[pages:2 * pages]
    o_ref = refs[2 * pages]
    qs_scr, m_scr, l_scr, acc_scr = refs[2 * pages + 1:]
    p = pl.program_id(1)
    n_rows = 2 * DIFF_HEADS * t_new
    head_rows = 2 * t_new

    @pl.when(p == 0)
    def _():
        q = q_ref[...] * (DIFF_DH ** -0.5)
        lane = lax.broadcasted_iota(jnp.int32, (t_new, DQ_W), 1) // DIFF_DH
        qs = jnp.concatenate([jnp.where(lane == g, q, 0.0) for g in range(2 * DIFF_HEADS)],
                             axis=0).astype(BF16)
        qs_scr[...] = qs
        s = _dot_nt(qs, kn_ref[...].astype(BF16))
        qpos = lax.broadcasted_iota(jnp.int32, (n_rows, t_new), 0) % t_new
        kpos = lax.broadcasted_iota(jnp.int32, (n_rows, t_new), 1)
        s = jnp.where(kpos <= qpos, s, -jnp.inf)
        m = jnp.max(s, axis=1, keepdims=True)
        e = jnp.exp(s - m)
        m_scr[...] = m
        l_scr[...] = jnp.sum(e, axis=1, keepdims=True)
        eb = e.astype(BF16)
        vn = vn_ref[...].astype(BF16)
        acc_scr[...] = jnp.concatenate(
            [_dot(eb[h * head_rows:(h + 1) * head_rows, :], vn[:, h * DIFF_DV:(h + 1) * DIFF_DV])
             for h in range(DIFF_HEADS)], axis=0)

    qs = qs_scr[...]
    s = jnp.concatenate([_dot(qs, k_refs[g][...].astype(BF16)) for g in range(pages)], axis=1)
    m_old = m_scr[...]
    m_new = jnp.maximum(m_old, jnp.max(s, axis=1, keepdims=True))
    alpha = jnp.exp(m_old - m_new)
    e = jnp.exp(s - m_new)
    l_scr[...] = alpha * l_scr[...] + jnp.sum(e, axis=1, keepdims=True)
    eb = e.astype(BF16)
    pv = []
    for h in range(DIFF_HEADS):
        rows = slice(h * head_rows, (h + 1) * head_rows)
        acc_h = None
        for g in range(pages):
            v_h = v_refs[g][pl.ds(h, PAGE_SIZE, stride=DIFF_HEADS), :].astype(BF16)
            t = _dot(eb[rows, g * PAGE_SIZE:(g + 1) * PAGE_SIZE], v_h)
            acc_h = t if acc_h is None else acc_h + t
        pv.append(acc_h)
    acc_scr[...] = alpha * acc_scr[...] + jnp.concatenate(pv, axis=0)
    m_scr[...] = m_new

    @pl.when(p == pl.num_programs(1) - 1)
    def _():
        out = acc_scr[...] / l_scr[...]
        lam = _lambda(lp_ref, lam_init)
        heads = []
        for h in range(DIFF_HEADS):
            r1 = h * head_rows
            r2 = r1 + t_new
            heads.append(out[r1:r1 + t_new, :] - lam * out[r2:r2 + t_new, :])
        o_ref[...] = jnp.concatenate(heads, axis=1)


def _attn_sample(q, k_new, v_new, cache_kt, cache_v, page_table, lam_params, lam_init, *, pages):
    db, t_new, _ = q.shape
    n_pages = page_table.shape[1]
    grid = (db, n_pages // pages)
    n_rows = 2 * DIFF_HEADS * t_new
    tok = pl.BlockSpec((None, t_new, DQ_W), lambda b, p, pt: (b, 0, 0))

    def page_spec(g):
        return pl.BlockSpec((None, DQ_W, PAGE_SIZE), lambda b, p, pt: (pt[b, p * pages + g], 0, 0))

    grid_spec = pltpu.PrefetchScalarGridSpec(
        num_scalar_prefetch=1,
        grid=grid,
        in_specs=[pl.BlockSpec((4, DIFF_DH), lambda b, p, pt: (0, 0)), tok, tok, tok]
        + [page_spec(g) for g in range(pages)] + [page_spec(g) for g in range(pages)],
        out_specs=tok,
        scratch_shapes=[pltpu.VMEM((n_rows, DQ_W), BF16), pltpu.VMEM((n_rows, 1), F32),
                        pltpu.VMEM((n_rows, 1), F32), pltpu.VMEM((n_rows, DIFF_DV), F32)],
    )
    return pl.pallas_call(
        functools.partial(_attn_sample_kernel, pages=pages, t_new=t_new, lam_init=lam_init),
        grid_spec=grid_spec,
        out_shape=jax.ShapeDtypeStruct((db, t_new, DV_W), F32),
        compiler_params=pltpu.CompilerParams(dimension_semantics=("arbitrary", "arbitrary"),
                                             vmem_limit_bytes=VMEM_LIMIT),
        name="attn_sample",
    )(page_table, lam_params, q, k_new, v_new, *([cache_kt] * pages), *([cache_v] * pages))


def _out_ffn_kernel(x_ref, go_ref, gg_ref, do_ref, p_ref, gn_ref, dn_ref, wout_ref, fn_ref,
                    wg_ref, wu_ref, wd_ref, pn_ref, wpg_ref, wpp_ref, y_ref, *, lam_init):
    def head_norm(o, g):
        return jnp.concatenate(
            [_rms(o[:, h * LANES:(h + 1) * LANES], g) for h in range(o.shape[1] // LANES)], axis=1)

    gg = gg_ref[...]
    go = head_norm(go_ref[...], gn_ref[...]) * (gg * jax.nn.sigmoid(gg))
    do = head_norm(do_ref[...], dn_ref[...]) * (1.0 - lam_init)
    mix = jnp.concatenate([go, do], axis=1).astype(BF16)
    h = x_ref[...] + _dot(mix, wout_ref[...])
    hn = _rms(h, fn_ref[...]).astype(BF16)
    gate = _dot(hn, wg_ref[...])
    f = (gate * jax.nn.sigmoid(gate)) * _dot(hn, wu_ref[...])
    h = h + _dot(f.astype(BF16), wd_ref[...])
    pg = jax.nn.sigmoid(_dot(_rms(h, pn_ref[...]).astype(BF16), wpg_ref[...]))
    y_ref[...] = h + pg * _dot(p_ref[...].astype(BF16), wpp_ref[...])


def _out_ffn(x, go, gg, do, p, w, lam_init, *, tm):
    n = x.shape[0]
    d_ff = w["w_ffn_gate"].shape[1]
    ple = p.shape[1]
    tok = lambda width: pl.BlockSpec((tm, width), lambda i: (i, 0))
    return pl.pallas_call(
        functools.partial(_out_ffn_kernel, lam_init=lam_init),
        grid=(n // tm,),
        in_specs=[tok(D_MODEL), tok(GV_W), tok(GV_W), tok(DV_W), tok(ple),
                  _const_spec((1, GLA_DV)), _const_spec((1, DIFF_DV)),
                  _const_spec((D_MODEL, D_MODEL)), _const_spec((1, D_MODEL)),
                  _const_spec((D_MODEL, d_ff)), _const_spec((D_MODEL, d_ff)),
                  _const_spec((d_ff, D_MODEL)), _const_spec((1, D_MODEL)),
                  _const_spec((D_MODEL, D_MODEL)), _const_spec((ple, D_MODEL))],
        out_specs=tok(D_MODEL),
        out_shape=jax.ShapeDtypeStruct((n, D_MODEL), F32),
        compiler_params=pltpu.CompilerParams(dimension_semantics=("arbitrary",),
                                             vmem_limit_bytes=VMEM_LIMIT),
        name="out_ffn",
    )(x, go, gg, do, p, w["gla_norm_g"], w["diff_norm_g"], w["w_out"], w["ffn_norm_g"],
      w["w_ffn_gate"], w["w_ffn_up"], w["w_ffn_down"], w["ple_norm_g"], w["w_ple_gate"],
      w["w_ple_proj"])


def _rope_tables(pos):
    half = DIFF_DH // 2
    inv = ROPE_THETA ** (-jnp.arange(half, dtype=F32) / half)
    ang = pos.astype(F32)[:, None] * inv[None, :]
    cos, sin = jnp.cos(ang), jnp.sin(ang)
    cos = jnp.concatenate([cos, cos], axis=1)
    sin = jnp.concatenate([-sin, sin], axis=1)
    reps = LANES // DIFF_DH
    return jnp.tile(cos, (1, reps)), jnp.tile(sin, (1, reps))


def _layer_weights(i, attn_norm_g, w_in, w_gk2, b_gk, q_norm_g, k_norm_g, lam_params, gla_norm_g,
                   diff_norm_g, w_out, ffn_norm_g, w_ffn_gate, w_ffn_up, w_ffn_down, ple_norm_g,
                   w_ple_gate, w_ple_proj):
    lr0 = 2 * GK_W + 2 * GV_W
    wi = w_in[i]
    w_lr = jnp.pad(wi[:, lr0:lr0 + GLA_GATE_RANK], ((0, 0), (0, LR_PAD - GLA_GATE_RANK)))
    gid = jnp.arange(DQ_W) // DIFF_DH
    row = lambda a: a.reshape(1, -1).astype(F32)
    return {
        "attn_norm_g": row(attn_norm_g[i]),
        "w_proj": jnp.concatenate([wi[:, :lr0], wi[:, lr0 + GLA_GATE_RANK:]], axis=1).astype(BF16),
        "w_lr": w_lr.astype(BF16),
        "w_gk2": jnp.pad(w_gk2[i], ((0, LR_PAD - GLA_GATE_RANK), (0, 0))).astype(BF16),
        "b_gk": row(b_gk[i]),
        "q_norm_g": row(jnp.tile(q_norm_g[i], DQ_W // DIFF_DH)),
        "k_norm_g": row(jnp.tile(k_norm_g[i], DQ_W // DIFF_DH)),
        "group_ind": (gid[:, None] == gid[None, :]).astype(BF16),
        "lam_params": lam_params[i].astype(F32),
        "gla_norm_g": row(gla_norm_g[i]),
        "diff_norm_g": row(diff_norm_g[i]),
        "w_out": w_out[i].astype(BF16),
        "ffn_norm_g": row(ffn_norm_g[i]),
        "w_ffn_gate": w_ffn_gate[i].astype(BF16),
        "w_ffn_up": w_ffn_up[i].astype(BF16),
        "w_ffn_down": w_ffn_down[i].astype(BF16),
        "ple_norm_g": row(ple_norm_g[i]),
        "w_ple_gate": w_ple_gate[i].astype(BF16),
        "w_ple_proj": w_ple_proj[i].astype(BF16),
    }


def kernel(x_prompt, x_sample, p_prompt, p_sample, cache_k, cache_v, state_gla, page_table, attn_norm_g, w_in, w_gk2, b_gk, q_norm_g, k_norm_g, lam_params, gla_norm_g, diff_norm_g, w_out, ffn_norm_g, w_ffn_gate, w_ffn_up, w_ffn_down, ple_norm_g, w_ple_gate, w_ple_proj):
    B, S, _ = x_prompt.shape
    DB, T, _ = x_sample.shape
    depth = w_in.shape[0]
    n_pool = cache_k.shape[1]
    n_pages = page_table.shape[1]
    past_len = n_pages * PAGE_SIZE

    tm_p = 512
    cos_p, sin_p = _rope_tables(jnp.arange(S))
    cos_s, sin_s = _rope_tables(jnp.tile(past_len + jnp.arange(T), DB))
    t_pad = GLA_GROUP
    cache_k2 = jnp.transpose(cache_k, (0, 1, 3, 4, 5, 2)).reshape(depth * n_pool, DQ_W, PAGE_SIZE)
    cache_v2 = cache_v.reshape(depth * n_pool, PAGE_SIZE * DIFF_HEADS, DIFF_DV)

    hp = x_prompt.reshape(B * S, D_MODEL)
    hs = x_sample.reshape(DB * T, D_MODEL)
    kp_l, vp_l, gp_l, ks_l, vs_l, gs_l = [], [], [], [], [], []
    for i in range(depth):
        lam_init = 0.8 - 0.6 * math.exp(-0.3 * i)
        w = _layer_weights(i, attn_norm_g, w_in, w_gk2, b_gk, q_norm_g, k_norm_g, lam_params,
                           gla_norm_g, diff_norm_g, w_out, ffn_norm_g, w_ffn_gate, w_ffn_up,
                           w_ffn_down, ple_norm_g, w_ple_gate, w_ple_proj)

        gq, gk, gv, gg, la, dq, dk, dv, dkt = _proj(hp, cos_p, sin_p, w, tm=tm_p,
                                                    pos_period_blocks=S // tm_p, seq=S)
        r3 = lambda a: a.reshape(B, S, a.shape[-1])
        s0 = jnp.zeros((B, GK_W, GLA_DV), F32)
        gla_o, gla_s = _gla(r3(gq), r3(gk), r3(gv), r3(la), s0, block=512)
        diff_o = _attn_prompt(r3(dq), r3(dk), r3(dv), w["lam_params"], lam_init, tq=1024)
        hp = _out_ffn(hp, gla_o.reshape(B * S, GV_W), gg, diff_o.reshape(B * S, DV_W),
                      p_prompt[i].reshape(B * S, -1), w, lam_init, tm=256)
        kp_l.append(jnp.transpose(dkt.reshape(B, DIFF_HEADS, 2, DIFF_DH, S), (0, 4, 1, 2, 3)))
        vp_l.append(dv.reshape(B, S, DIFF_HEADS, DIFF_DV))
        gp_l.append(gla_s.reshape(B, GLA_HEADS, GLA_DK, GLA_DV))

        gq, gk, gv, gg, la, dq, dk, dv = _proj(hs, cos_s, sin_s, w, tm=DB * T, pos_period_blocks=1)
        r3 = lambda a: a.reshape(DB, T, a.shape[-1])
        padt = lambda a: jnp.pad(r3(a), ((0, 0), (0, t_pad - T), (0, 0)))
        s0 = state_gla[i].reshape(DB, GK_W, GLA_DV)
        gla_o, gla_s = _gla(padt(gq), padt(gk), padt(gv), padt(la), s0, block=t_pad)
        gla_o = gla_o[:, :T]
        diff_o = _attn_sample(r3(dq), r3(dk), r3(dv), cache_k2, cache_v2, page_table + i * n_pool,
                              w["lam_params"], lam_init, pages=8)
        hs = _out_ffn(hs, gla_o.reshape(DB * T, GV_W), gg, diff_o.reshape(DB * T, DV_W),
                      p_sample[i].reshape(DB * T, -1), w, lam_init, tm=DB * T)
        ks_l.append(dk.reshape(DB, T, DIFF_HEADS, 2, DIFF_DH))
        vs_l.append(dv.reshape(DB, T, DIFF_HEADS, DIFF_DV))
        gs_l.append(gla_s.reshape(DB, GLA_HEADS, GLA_DK, GLA_DV))

    return (hp.reshape(B, S, D_MODEL), hs.reshape(DB, T, D_MODEL),
            jnp.stack(kp_l), jnp.stack(vp_l), jnp.stack(gp_l),
            jnp.stack(ks_l), jnp.stack(vs_l), jnp.stack(gs_l))
```

```python
import functools
import math

import jax
import jax.numpy as jnp
import numpy as np
from jax import lax
from jax.experimental import pallas as pl
from jax.experimental.pallas import tpu as pltpu

F32 = jnp.float32
BF16 = jnp.bfloat16

D_MODEL = 1024
GLA_HEADS = 4
GLA_DV = 128
GLA_DK = 64
GLA_GATE_RANK = 16
GLA_GATE_NORMALIZER = 16.0
GLA_CHUNK = 64
DIFF_HEADS = 4
DIFF_DV = 128
DIFF_DH = 64
ROPE_THETA = 10000.0
PAGE_SIZE = 128
EPS = 1e-6
GK_W = GLA_HEADS * GLA_DK
GV_W = GLA_HEADS * GLA_DV
DQ_W = DIFF_HEADS * 2 * DIFF_DH
DV_W = DIFF_HEADS * DIFF_DV
LANES = 128
LR_PAD = LANES
PROJ_W = 2 * GK_W + 2 * GV_W + 2 * DQ_W + DV_W
VMEM_LIMIT = 56 * 1024 * 1024


def _dot(a, b):
    return jnp.dot(a, b, preferred_element_type=F32)


def _dot_nt(a, b):
    return lax.dot_general(a, b, (((1,), (1,)), ((), ())), preferred_element_type=F32)


def _dot_tn(a, b):
    return lax.dot_general(a, b, (((0,), (0,)), ((), ())), preferred_element_type=F32)


def _rms(x, g):
    return x * lax.rsqrt(jnp.mean(x * x, axis=-1, keepdims=True) + EPS) * g


def _const_spec(shape):
    nd = len(shape)
    return pl.BlockSpec(shape, lambda *_: (0,) * nd, pipeline_mode=pl.Buffered(1))


def _proj_kernel(x_ref, ng_ref, w_ref, wlr_ref, wgk2_ref, bgk_ref, qg_ref, kg_ref, cos_ref,
                 sin_ref, ind_ref, gq_ref, gk_ref, gv_ref, gg_ref, la_ref, dq_ref, *kv_refs,
                 feature_major):
    x = x_ref[...]
    xn = _rms(x, ng_ref[...]).astype(BF16)
    qk = _dot(xn, w_ref[:, :2 * DQ_W])
    dq = qk[:, :DQ_W]
    dk = qk[:, DQ_W:]

    tm = x.shape[0]
    cos = jnp.concatenate([cos_ref[...]] * (DQ_W // LANES), axis=1)
    sin = jnp.concatenate([sin_ref[...]] * (DQ_W // LANES), axis=1)
    lane = lax.broadcasted_iota(jnp.int32, (tm, DQ_W), 1)
    upper = (lane & (DIFF_DH // 2)) != 0
    ind = ind_ref[...]

    def norm_rope(y, g):
        ms = jnp.concatenate(
            [_dot((y[:, t * LANES:(t + 1) * LANES] ** 2).astype(BF16), ind)
             for t in range(DQ_W // LANES)], axis=1) * (1.0 / DIFF_DH)
        yn = y * lax.rsqrt(ms + EPS) * g
        partner = jnp.where(upper, pltpu.roll(yn, DIFF_DH // 2, 1),
                            pltpu.roll(yn, DQ_W - DIFF_DH // 2, 1))
        return yn * cos + partner * sin

    dq_ref[...] = norm_rope(dq, qg_ref[...])
    dk = norm_rope(dk, kg_ref[...])
    if feature_major:
        dkb_ref, dkt_ref, dv4_ref, dvt_ref = kv_refs
        dkb_ref[...] = dk.astype(BF16)
        dkt_ref[...] = dk.T
    else:
        dk_ref, dv_ref = kv_refs
        dk_ref[...] = dk

    glr = _dot(xn, wlr_ref[...])
    z = _dot(glr.astype(BF16), wgk2_ref[...]) + bgk_ref[...]
    logsig = jnp.minimum(z, 0.0) - jnp.log(1.0 + jnp.exp(-jnp.abs(z)))
    la_ref[...] = logsig * (1.0 / GLA_GATE_NORMALIZER)

    rest = _dot(xn, w_ref[:, 2 * DQ_W:])
    o = 0
    gq_ref[...] = rest[:, o:o + GK_W] * (GLA_DK ** -0.5); o += GK_W
    gk_ref[...] = rest[:, o:o + GK_W]; o += GK_W
    gv_ref[...] = rest[:, o:o + GV_W]; o += GV_W
    gg_ref[...] = rest[:, o:o + GV_W]; o += GV_W
    dv = rest[:, o:o + DV_W]
    if feature_major:
        for h in range(DIFF_HEADS):
            dv4_ref[pl.ds(h, tm, stride=DIFF_HEADS), :] = dv[:, h * DIFF_DV:(h + 1) * DIFF_DV]
        dvt_ref[...] = dv.T.astype(BF16)
    else:
        dv_ref[...] = dv


def _proj(x, pos_cos, pos_sin, w, *, tm, pos_period_blocks, seq=None):
    n = x.shape[0]
    grid = (n // tm,)
    tok = lambda width: pl.BlockSpec((tm, width), lambda i: (i, 0))
    pos = pl.BlockSpec((tm, LANES), lambda i: (i % pos_period_blocks, 0))
    out_widths = (GK_W, GK_W, GV_W, GV_W, GK_W, DQ_W)
    out_specs = [tok(wd) for wd in out_widths]
    out_shape = [jax.ShapeDtypeStruct((n, wd), F32) for wd in out_widths]
    if seq is None:
        out_specs += [tok(DQ_W), tok(DV_W)]
        out_shape += [jax.ShapeDtypeStruct((n, DQ_W), F32), jax.ShapeDtypeStruct((n, DV_W), F32)]
    else:
        per_seq = seq // tm
        fm = lambda width: pl.BlockSpec((None, width, tm), lambda i: (i // per_seq, 0, i % per_seq))
        out_specs += [tok(DQ_W), fm(DQ_W),
                      pl.BlockSpec((tm * DIFF_HEADS, DIFF_DV), lambda i: (i, 0)), fm(DV_W)]
        out_shape += [jax.ShapeDtypeStruct((n, DQ_W), BF16),
                      jax.ShapeDtypeStruct((n // seq, DQ_W, seq), F32),
                      jax.ShapeDtypeStruct((n * DIFF_HEADS, DIFF_DV), F32),
                      jax.ShapeDtypeStruct((n // seq, DV_W, seq), BF16)]
    return pl.pallas_call(
        functools.partial(_proj_kernel, feature_major=seq is not None),
        grid=grid,
        in_specs=[tok(D_MODEL), _const_spec((1, D_MODEL)), _const_spec((D_MODEL, PROJ_W)),
                  _const_spec((D_MODEL, LR_PAD)), _const_spec((LR_PAD, GK_W)),
                  _const_spec((1, GK_W)), _const_spec((1, DQ_W)), _const_spec((1, DQ_W)),
                  pos, pos, _const_spec((LANES, LANES))],
        out_specs=out_specs,
        out_shape=out_shape,
        compiler_params=pltpu.CompilerParams(dimension_semantics=("arbitrary",),
                                             vmem_limit_bytes=VMEM_LIMIT),
        name="proj",
    )(x, w["attn_norm_g"], w["w_proj"], w["w_lr"], w["w_gk2"], w["b_gk"], w["q_norm_g"],
      w["k_norm_g"], pos_cos, pos_sin, w["group_ind"])


GLA_GROUP = 128
GLA_LEVELS = GLA_CHUNK.bit_length() - 1


def _gla_constants():
    n = GLA_GROUP
    t = np.arange(n)[:, None]
    u = np.arange(n)[None, :]
    mats = [((t // GLA_CHUNK == u // GLA_CHUNK) & (u <= t)).astype(np.float32)]
    masks = []
    for lvl in range(GLA_LEVELS):
        h = 1 << lvl
        ref = (t // (2 * h)) * (2 * h) + h - 1
        upper = (t & h) != 0
        mats.append(np.where(upper & (u > ref) & (u <= t), 1.0, 0.0)
                    - np.where(~upper & (u > t) & (u <= ref), 1.0, 0.0))
        masks.append(((t // (2 * h) == u // (2 * h)) & upper & ((u & h) == 0)).astype(np.float32))
    head = np.arange(GK_W)[:, None] // GLA_DK == np.arange(GV_W)[None, :] // GLA_DV
    return (jnp.asarray(np.concatenate(mats, 0), BF16), jnp.asarray(np.concatenate(masks, 0), BF16),
            jnp.asarray(head.astype(np.float32), BF16))


def _gla_kernel(q_ref, k_ref, v_ref, la_ref, s0_ref, mexp_ref, mask_ref, expand_ref, o_ref,
                sout_ref, s_scr, ex_scr, a_scr, u_scr, d_scr, sb_scr, *, n_groups):
    j = pl.program_id(1)
    n = GLA_GROUP
    pair_w = 2 * GLA_DK
    n_pairs = GLA_HEADS // 2
    per_group = n // GLA_CHUNK

    @pl.when(j == 0)
    def _():
        s_scr[...] = s0_ref[...]

    def head_only(x, h):
        lane = lax.broadcasted_iota(jnp.int32, x.shape, 1)
        keep = lane < GLA_DK if h % 2 == 0 else lane >= GLA_DK
        return jnp.where(keep, x, 0.0)

    grp = lambda g: slice(g * n, (g + 1) * n)
    chunk_rows = lambda c: slice(c * GLA_CHUNK, (c + 1) * GLA_CHUNK)

    def chunk_b(c):
        lo = (c % per_group) * GLA_CHUNK
        return ex_scr[c // per_group, lo:lo + GLA_CHUNK, :]

    for g in range(n_groups):
        la = la_ref[grp(g), :]
        la_hi = la.astype(BF16)
        la_lo = (la - la_hi.astype(F32)).astype(BF16)
        ex_scr[g, :n, :] = _dot(mexp_ref[:n, :], la_hi) + _dot(mexp_ref[:n, :], la_lo)
        ex_scr[g, n:, :] = _dot(mexp_ref[n:, :], la_hi)

    for g in range(n_groups):
        q = q_ref[grp(g), :]
        k = k_ref[grp(g), :]
        a = [None] * GLA_HEADS
        for lvl in range(GLA_LEVELS):
            d = ex_scr[g, (lvl + 1) * n:(lvl + 2) * n, :]
            ql = q * jnp.exp(jnp.minimum(d, 0.0))
            kl = (k * jnp.exp(jnp.minimum(-d, 0.0))).astype(BF16)
            msk = mask_ref[lvl * n:(lvl + 1) * n, :] > 0
            for h in range(GLA_HEADS):
                tile = slice((h // 2) * pair_w, (h // 2 + 1) * pair_w)
                t = _dot_nt(head_only(ql[:, tile], h).astype(BF16), kl[:, tile])
                t = jnp.where(msk, t, 0.0)
                a[h] = t if a[h] is None else a[h] + t
        for h in range(GLA_HEADS):
            a_scr[g, h] = a[h].astype(BF16)

    for g in range(n_groups):
        v = v_ref[grp(g), :]
        vb = v.astype(BF16)
        o = _dot((q_ref[grp(g), :] * k_ref[grp(g), :]).astype(BF16), expand_ref[...]) * v
        o_ref[grp(g), :] = o + jnp.concatenate(
            [_dot(a_scr[g, h], vb[:, h * GLA_DV:(h + 1) * GLA_DV]) for h in range(GLA_HEADS)],
            axis=1)

    for c in range(n_groups * per_group):
        r = chunk_rows(c)
        bc = chunk_b(c)
        b_last = bc[GLA_CHUNK - 1:GLA_CHUNK, :]
        ke = (k_ref[r, :] * jnp.exp(b_last - bc)).astype(BF16)
        vb = v_ref[r, :].astype(BF16)
        decay = jnp.exp(jnp.broadcast_to(b_last, (GLA_DV, GK_W))).T
        for pair in range(n_pairs):
            tile = slice(pair * pair_w, (pair + 1) * pair_w)
            upd = []
            for h in (2 * pair, 2 * pair + 1):
                kv = _dot_tn(ke[:, tile], vb[:, h * GLA_DV:(h + 1) * GLA_DV])
                upd.append(kv[(h % 2) * GLA_DK:(h % 2 + 1) * GLA_DK])
            u_scr[c, pair] = jnp.concatenate(upd, axis=0)
            d_scr[c, pair] = decay[tile, :]

    for c in range(n_groups * per_group):
        for pair in range(n_pairs):
            tile = slice(pair * pair_w, (pair + 1) * pair_w)
            s_pair = s_scr[tile, :]
            sb_scr[c, pair] = s_pair.astype(BF16)
            s_scr[tile, :] = d_scr[c, pair] * s_pair + u_scr[c, pair]

    for c in range(n_groups * per_group):
        r = chunk_rows(c)
        qe = q_ref[r, :] * jnp.exp(chunk_b(c))
        heads = []
        for h in range(GLA_HEADS):
            tile = slice((h // 2) * pair_w, (h // 2 + 1) * pair_w)
            heads.append(_dot(head_only(qe[:, tile], h).astype(BF16), sb_scr[c, h // 2]))
        o_ref[r, :] += jnp.concatenate(heads, axis=1)

    @pl.when(j == pl.num_programs(1) - 1)
    def _():
        sout_ref[...] = s_scr[...]


def _gla(q, k, v, la, s0, *, block):
    b, t, _ = q.shape
    assert t % block == 0 and block % GLA_GROUP == 0
    mexp, masks, expand = _gla_constants()
    n_groups = block // GLA_GROUP
    n_chunks = block // GLA_CHUNK
    n_pairs = GLA_HEADS // 2
    grid = (b, t // block)
    tok = lambda width: pl.BlockSpec((None, block, width), lambda i, j: (i, j, 0))
    st = pl.BlockSpec((None, GK_W, GLA_DV), lambda i, j: (i, 0, 0))
    return pl.pallas_call(
        functools.partial(_gla_kernel, n_groups=n_groups),
        grid=grid,
        in_specs=[tok(GK_W), tok(GK_W), tok(GV_W), tok(GK_W), st, _const_spec(mexp.shape),
                  _const_spec(masks.shape), _const_spec(expand.shape)],
        out_specs=[tok(GV_W), st],
        out_shape=[jax.ShapeDtypeStruct((b, t, GV_W), F32),
                   jax.ShapeDtypeStruct((b, GK_W, GLA_DV), F32)],
        scratch_shapes=[pltpu.VMEM((GK_W, GLA_DV), F32),
                        pltpu.VMEM((n_groups, (GLA_LEVELS + 1) * GLA_GROUP, GK_W), F32),
                        pltpu.VMEM((n_groups, GLA_HEADS, GLA_GROUP, GLA_GROUP), BF16),
                        pltpu.VMEM((n_chunks, n_pairs, 2 * GLA_DK, GLA_DV), F32),
                        pltpu.VMEM((n_chunks, n_pairs, 2 * GLA_DK, GLA_DV), F32),
                        pltpu.VMEM((n_chunks, n_pairs, 2 * GLA_DK, GLA_DV), BF16)],
        compiler_params=pltpu.CompilerParams(dimension_semantics=("arbitrary", "arbitrary"),
                                             vmem_limit_bytes=VMEM_LIMIT),
        name="gla",
    )(q, k, v, la, s0, mexp, masks, expand)


def _lambda(lp_ref, lam_init):
    lp = lp_ref[...]
    s1 = jnp.sum(lp[0:1, :] * lp[1:2, :], axis=1, keepdims=True)
    s2 = jnp.sum(lp[2:3, :] * lp[3:4, :], axis=1, keepdims=True)
    return jnp.exp(s1) - jnp.exp(s2) + lam_init


_RELAYOUT_ROWS = 512
ATTN_CHAIN = 256
ATTN_SLOTS = 16
ONES_ROWS = 16


def _attn_prompt_kernel(lp_ref, q_ref, kb_ref, vt_ref, o_ref, vt_scr, qst_scr, st_scr, m_scr,
                        acc_scr, *, tq, seq, lam_init):
    i = pl.program_id(2)
    cw = ATTN_CHAIN
    per_sub = tq // cw
    n_chains = 2 * per_sub

    @pl.when(i == 0)
    def _():
        vt_scr[:DIFF_DV, :] = vt_ref[...]
        vt_scr[DIFF_DV:, :] = jnp.ones((ONES_ROWS, seq), BF16)

    feat = lax.broadcasted_iota(jnp.int32, (2 * DIFF_DH, _RELAYOUT_ROWS), 0)
    first = feat < DIFF_DH
    for c in range(tq // _RELAYOUT_ROWS):
        r = slice(c * _RELAYOUT_ROWS, (c + 1) * _RELAYOUT_ROWS)
        qt = (q_ref[r, :] * (DIFF_DH ** -0.5 * math.log2(math.e))).T
        qst_scr[:, r] = jnp.where(first, qt, 0.0).astype(BF16)
        qst_scr[:, tq + c * _RELAYOUT_ROWS:tq + (c + 1) * _RELAYOUT_ROWS] = (
            jnp.where(first, 0.0, qt).astype(BF16))

    m_scr[...] = jnp.full(m_scr.shape, -jnp.inf, F32)
    acc_scr[...] = jnp.zeros(acc_scr.shape, F32)

    kpos = lax.broadcasted_iota(jnp.int32, (cw, cw), 0)
    qpos = lax.broadcasted_iota(jnp.int32, (cw, cw), 1)
    causal = kpos <= qpos

    def scores(slot, r, c):
        st_scr[:, slot * cw:(slot + 1) * cw] = _dot(kb_ref[r, :], qst_scr[:, c * cw:(c + 1) * cw])

    def softmax_pv(slot, r, c, masked):
        cols = slice(c * cw, (c + 1) * cw)
        st = st_scr[:, slot * cw:(slot + 1) * cw]
        if masked:
            st = jnp.where(causal, st, -jnp.inf)
        m_old = m_scr[:, cols]
        m_new = jnp.maximum(m_old, jnp.max(st, axis=0, keepdims=True))
        p = jnp.exp2(st - m_new).astype(BF16)
        acc_scr[:, cols] = jnp.exp2(m_old - m_new) * acc_scr[:, cols] + _dot(vt_scr[:, r], p)
        m_scr[:, cols] = m_new

    def run(items):
        assert len(items) <= ATTN_SLOTS
        for slot, (r, c, _) in enumerate(items):
            scores(slot, r, c)
        for slot, (r, c, masked) in enumerate(items):
            softmax_pv(slot, r, c, masked)

    blocks_per_trip = ATTN_SLOTS // n_chains

    def body(j, carry):
        items = []
        for u in range(blocks_per_trip):
            r = pl.ds(pl.multiple_of((j * blocks_per_trip + u) * cw, cw), cw)
            items += [(r, c, False) for c in range(n_chains)]
        run(items)
        return carry

    lax.fori_loop(0, i * per_sub // blocks_per_trip, body, 0)

    items = []
    for jj in range(per_sub):
        r = pl.ds(pl.multiple_of(i * tq + jj * cw, cw), cw)
        block = [(r, c, c % per_sub == jj) for c in range(n_chains) if c % per_sub >= jj]
        if len(items) + len(block) > ATTN_SLOTS:
            run(items)
            items = []
        items += block
    run(items)

    lam = _lambda(lp_ref, lam_init)
    for c in range(per_sub):
        c1 = slice(c * cw, (c + 1) * cw)
        c2 = slice(tq + c * cw, tq + (c + 1) * cw)
        o1 = acc_scr[:DIFF_DV, c1] / acc_scr[DIFF_DV:DIFF_DV + 1, c1]
        o2 = acc_scr[:DIFF_DV, c2] / acc_scr[DIFF_DV:DIFF_DV + 1, c2]
        o_ref[c1, :] = (o1 - lam * o2).T


def _attn_prompt(q, kb, vt, lam_params, lam_init, *, tq):
    b, s, _ = q.shape
    per_sub = tq // ATTN_CHAIN
    assert tq % ATTN_CHAIN == 0 and ATTN_SLOTS % (2 * per_sub) == 0
    assert per_sub % (ATTN_SLOTS // (2 * per_sub)) == 0
    grid = (b, DIFF_HEADS, s // tq)
    width = 2 * DIFF_DH
    qspec = pl.BlockSpec((None, tq, width), lambda bi, h, i: (bi, i, h))
    kspec = pl.BlockSpec((None, s, width), lambda bi, h, i: (bi, 0, h))
    vtspec = pl.BlockSpec((None, DIFF_DV, s), lambda bi, h, i: (bi, h, 0))
    return pl.pallas_call(
        functools.partial(_attn_prompt_kernel, tq=tq, seq=s, lam_init=lam_init),
        grid=grid,
        in_specs=[_const_spec((4, DIFF_DH)), qspec, kspec, vtspec],
        out_specs=qspec,
        out_shape=jax.ShapeDtypeStruct((b, s, DV_W), F32),
        scratch_shapes=[pltpu.VMEM((DIFF_DV + ONES_ROWS, s), BF16),
                        pltpu.VMEM((width, 2 * tq), BF16),
                        pltpu.VMEM((ATTN_CHAIN, ATTN_SLOTS * ATTN_CHAIN), F32),
                        pltpu.VMEM((1, 2 * tq), F32),
                        pltpu.VMEM((DIFF_DV + ONES_ROWS, 2 * tq), F32)],
        compiler_params=pltpu.CompilerParams(
            dimension_semantics=("arbitrary", "arbitrary", "arbitrary"),
            vmem_limit_bytes=VMEM_LIMIT),
        name="attn_prompt",
    )(lam_params, q, kb, vt)


def _attn_sample_kernel(pt_ref, lp_ref, q_ref, kn_ref, vn_ref, *refs, pages, t_new, lam_init):
    k_refs = refs[:pages]
    v_refs = refs[pages:2 * pages]
    o_ref = refs[2 * pages]
    qs_scr, m_scr, l_scr, acc_scr = refs[2 * pages + 1:]
    p = pl.program_id(1)
    n_rows = 2 * DIFF_HEADS * t_new
    head_rows = 2 * t_new

    @pl.when(p == 0)
    def _():
        q = q_ref[...] * (DIFF_DH ** -0.5)
        lane = lax.broadcasted_iota(jnp.int32, (t_new, DQ_W), 1) // DIFF_DH
        qs = jnp.concatenate([jnp.where(lane == g, q, 0.0) for g in range(2 * DIFF_HEADS)],
                             axis=0).astype(BF16)
        qs_scr[...] = qs
        s = _dot_nt(qs, kn_ref[...].astype(BF16))
        qpos = lax.broadcasted_iota(jnp.int32, (n_rows, t_new), 0) % t_new
        kpos = lax.broadcasted_iota(jnp.int32, (n_rows, t_new), 1)
        s = jnp.where(kpos <= qpos, s, -jnp.inf)
        m = jnp.max(s, axis=1, keepdims=True)
        e = jnp.exp(s - m)
        m_scr[...] = m
        l_scr[...] = jnp.sum(e, axis=1, keepdims=True)
        eb = e.astype(BF16)
        vn = vn_ref[...].astype(BF16)
        acc_scr[...] = jnp.concatenate(
            [_dot(eb[h * head_rows:(h + 1) * head_rows, :], vn[:, h * DIFF_DV:(h + 1) * DIFF_DV])
             for h in range(DIFF_HEADS)], axis=0)

    qs = qs_scr[...]
    s = jnp.concatenate([_dot(qs, k_refs[g][...].astype(BF16)) for g in range(pages)], axis=1)
    m_old = m_scr[...]
    m_new = jnp.maximum(m_old, jnp.max(s, axis=1, keepdims=True))
    alpha = jnp.exp(m_old - m_new)
    e = jnp.exp(s - m_new)
    l_scr[...] = alpha * l_scr[...] + jnp.sum(e, axis=1, keepdims=True)
    eb = e.astype(BF16)
    pv = []
    for h in range(DIFF_HEADS):
        rows = slice(h * head_rows, (h + 1) * head_rows)
        acc_h = None
        for g in range(pages):
            v_h = v_refs[g][pl.ds(h, PAGE_SIZE, stride=DIFF_HEADS), :].astype(BF16)
            t = _dot(eb[rows, g * PAGE_SIZE:(g + 1) * PAGE_SIZE], v_h)
            acc_h = t if acc_h is None else acc_h + t
        pv.append(acc_h)
    acc_scr[...] = alpha * acc_scr[...] + jnp.concatenate(pv, axis=0)
    m_scr[...] = m_new

    @pl.when(p == pl.num_programs(1) - 1)
    def _():
        out = acc_scr[...] / l_scr[...]
        lam = _lambda(lp_ref, lam_init)
        heads = []
        for h in range(DIFF_HEADS):
            r1 = h * head_rows
            r2 = r1 + t_new
            heads.append(out[r1:r1 + t_new, :] - lam * out[r2:r2 + t_new, :])
        o_ref[...] = jnp.concatenate(heads, axis=1)


def _attn_sample(q, k_new, v_new, cache_kt, cache_v, page_table, lam_params, lam_init, *, pages):
    db, t_new, _ = q.shape
    n_pages = page_table.shape[1]
    grid = (db, n_pages // pages)
    n_rows = 2 * DIFF_HEADS * t_new
    tok = pl.BlockSpec((None, t_new, DQ_W), lambda b, p, pt: (b, 0, 0))

    def page_spec(g):
        return pl.BlockSpec((None, DQ_W, PAGE_SIZE), lambda b, p, pt: (pt[b, p * pages + g], 0, 0))

    grid_spec = pltpu.PrefetchScalarGridSpec(
        num_scalar_prefetch=1,
        grid=grid,
        in_specs=[pl.BlockSpec((4, DIFF_DH), lambda b, p, pt: (0, 0)), tok, tok, tok]
        + [page_spec(g) for g in range(pages)] + [page_spec(g) for g in range(pages)],
        out_specs=tok,
        scratch_shapes=[pltpu.VMEM((n_rows, DQ_W), BF16), pltpu.VMEM((n_rows, 1), F32),
                        pltpu.VMEM((n_rows, 1), F32), pltpu.VMEM((n_rows, DIFF_DV), F32)],
    )
    return pl.pallas_call(
        functools.partial(_attn_sample_kernel, pages=pages, t_new=t_new, lam_init=lam_init),
        grid_spec=grid_spec,
        out_shape=jax.ShapeDtypeStruct((db, t_new, DV_W), F32),
        compiler_params=pltpu.CompilerParams(dimension_semantics=("arbitrary", "arbitrary"),
                                             vmem_limit_bytes=VMEM_LIMIT),
        name="attn_sample",
    )(page_table, lam_params, q, k_new, v_new, *([cache_kt] * pages), *([cache_v] * pages))


def _out_ffn_kernel(x_ref, go_ref, gg_ref, do_ref, p_ref, gn_ref, dn_ref, wout_ref, fn_ref,
                    wg_ref, wu_ref, wd_ref, pn_ref, wpg_ref, wpp_ref, y_ref, *, lam_init):
    def head_norm(o, g):
        return jnp.concatenate(
            [_rms(o[:, h * LANES:(h + 1) * LANES], g) for h in range(o.shape[1] // LANES)], axis=1)

    gg = gg_ref[...]
    go = head_norm(go_ref[...], gn_ref[...]) * (gg * jax.nn.sigmoid(gg))
    do = head_norm(do_ref[...], dn_ref[...]) * (1.0 - lam_init)
    mix = jnp.concatenate([go, do], axis=1).astype(BF16)
    h = x_ref[...] + _dot(mix, wout_ref[...])
    hn = _rms(h, fn_ref[...]).astype(BF16)
    gate = _dot(hn, wg_ref[...])
    f = (gate * jax.nn.sigmoid(gate)) * _dot(hn, wu_ref[...])
    h = h + _dot(f.astype(BF16), wd_ref[...])
    pg = jax.nn.sigmoid(_dot(_rms(h, pn_ref[...]).astype(BF16), wpg_ref[...]))
    y_ref[...] = h + pg * _dot(p_ref[...].astype(BF16), wpp_ref[...])


def _out_ffn(x, go, gg, do, p, w, lam_init, *, tm):
    n = x.shape[0]
    d_ff = w["w_ffn_gate"].shape[1]
    ple = p.shape[1]
    tok = lambda width: pl.BlockSpec((tm, width), lambda i: (i, 0))
    return pl.pallas_call(
        functools.partial(_out_ffn_kernel, lam_init=lam_init),
        grid=(n // tm,),
        in_specs=[tok(D_MODEL), tok(GV_W), tok(GV_W), tok(DV_W), tok(ple),
                  _const_spec((1, GLA_DV)), _const_spec((1, DIFF_DV)),
                  _const_spec((D_MODEL, D_MODEL)), _const_spec((1, D_MODEL)),
                  _const_spec((D_MODEL, d_ff)), _const_spec((D_MODEL, d_ff)),
                  _const_spec((d_ff, D_MODEL)), _const_spec((1, D_MODEL)),
                  _const_spec((D_MODEL, D_MODEL)), _const_spec((ple, D_MODEL))],
        out_specs=tok(D_MODEL),
        out_shape=jax.ShapeDtypeStruct((n, D_MODEL), F32),
        compiler_params=pltpu.CompilerParams(dimension_semantics=("arbitrary",),
                                             vmem_limit_bytes=VMEM_LIMIT),
        name="out_ffn",
    )(x, go, gg, do, p, w["gla_norm_g"], w["diff_norm_g"], w["w_out"], w["ffn_norm_g"],
      w["w_ffn_gate"], w["w_ffn_up"], w["w_ffn_down"], w["ple_norm_g"], w["w_ple_gate"],
      w["w_ple_proj"])


def _rope_tables(pos):
    half = DIFF_DH // 2
    inv = ROPE_THETA ** (-jnp.arange(half, dtype=F32) / half)
    ang = pos.astype(F32)[:, None] * inv[None, :]
    cos, sin = jnp.cos(ang), jnp.sin(ang)
    cos = jnp.concatenate([cos, cos], axis=1)
    sin = jnp.concatenate([-sin, sin], axis=1)
    reps = LANES // DIFF_DH
    return jnp.tile(cos, (1, reps)), jnp.tile(sin, (1, reps))


def _layer_weights(i, attn_norm_g, w_in, w_gk2, b_gk, q_norm_g, k_norm_g, lam_params, gla_norm_g,
                   diff_norm_g, w_out, ffn_norm_g, w_ffn_gate, w_ffn_up, w_ffn_down, ple_norm_g,
                   w_ple_gate, w_ple_proj):
    lr0 = 2 * GK_W + 2 * GV_W
    wi = w_in[i]
    w_lr = jnp.pad(wi[:, lr0:lr0 + GLA_GATE_RANK], ((0, 0), (0, LR_PAD - GLA_GATE_RANK)))
    gid = jnp.arange(LANES) // DIFF_DH
    dq0 = lr0 + GLA_GATE_RANK
    row = lambda a: a.reshape(1, -1).astype(F32)
    return {
        "attn_norm_g": row(attn_norm_g[i]),
        "w_proj": jnp.concatenate([wi[:, dq0:dq0 + 2 * DQ_W], wi[:, :lr0], wi[:, dq0 + 2 * DQ_W:]],
                                  axis=1).astype(BF16),
        "w_lr": w_lr.astype(BF16),
        "w_gk2": jnp.pad(w_gk2[i], ((0, LR_PAD - GLA_GATE_RANK), (0, 0))).astype(BF16),
        "b_gk": row(b_gk[i]),
        "q_norm_g": row(jnp.tile(q_norm_g[i], DQ_W // DIFF_DH)),
        "k_norm_g": row(jnp.tile(k_norm_g[i], DQ_W // DIFF_DH)),
        "group_ind": (gid[:, None] == gid[None, :]).astype(BF16),
        "lam_params": lam_params[i].astype(F32),
        "gla_norm_g": row(gla_norm_g[i]),
        "diff_norm_g": row(diff_norm_g[i]),
        "w_out": w_out[i].astype(BF16),
        "ffn_norm_g": row(ffn_norm_g[i]),
        "w_ffn_gate": w_ffn_gate[i].astype(BF16),
        "w_ffn_up": w_ffn_up[i].astype(BF16),
        "w_ffn_down": w_ffn_down[i].astype(BF16),
        "ple_norm_g": row(ple_norm_g[i]),
        "w_ple_gate": w_ple_gate[i].astype(BF16),
        "w_ple_proj": w_ple_proj[i].astype(BF16),
    }


def kernel(x_prompt, x_sample, p_prompt, p_sample, cache_k, cache_v, state_gla, page_table, attn_norm_g, w_in, w_gk2, b_gk, q_norm_g, k_norm_g, lam_params, gla_norm_g, diff_norm_g, w_out, ffn_norm_g, w_ffn_gate, w_ffn_up, w_ffn_down, ple_norm_g, w_ple_gate, w_ple_proj):
    B, S, _ = x_prompt.shape
    DB, T, _ = x_sample.shape
    depth = w_in.shape[0]
    n_pool = cache_k.shape[1]
    n_pages = page_table.shape[1]
    past_len = n_pages * PAGE_SIZE

    tm_p = 512
    cos_p, sin_p = _rope_tables(jnp.arange(S))
    cos_s, sin_s = _rope_tables(jnp.tile(past_len + jnp.arange(T), DB))
    t_pad = GLA_GROUP
    cache_k2 = jnp.transpose(cache_k, (0, 1, 3, 4, 5, 2)).reshape(depth * n_pool, DQ_W, PAGE_SIZE)
    cache_v2 = cache_v.reshape(depth * n_pool, PAGE_SIZE * DIFF_HEADS, DIFF_DV)

    hp = x_prompt.reshape(B * S, D_MODEL)
    hs = x_sample.reshape(DB * T, D_MODEL)
    kp_l, vp_l, gp_l, ks_l, vs_l, gs_l = [], [], [], [], [], []
    for i in range(depth):
        lam_init = 0.8 - 0.6 * math.exp(-0.3 * i)
        w = _layer_weights(i, attn_norm_g, w_in, w_gk2, b_gk, q_norm_g, k_norm_g, lam_params,
                           gla_norm_g, diff_norm_g, w_out, ffn_norm_g, w_ffn_gate, w_ffn_up,
                           w_ffn_down, ple_norm_g, w_ple_gate, w_ple_proj)

        gq, gk, gv, gg, la, dq, dkb, dkt, dv4, dvt = _proj(hp, cos_p, sin_p, w, tm=tm_p,
                                                           pos_period_blocks=S // tm_p, seq=S)
        r3 = lambda a: a.reshape(B, S, a.shape[-1])
        s0 = jnp.zeros((B, GK_W, GLA_DV), F32)
        gla_o, gla_s = _gla(r3(gq), r3(gk), r3(gv), r3(la), s0, block=512)
        diff_o = _attn_prompt(r3(dq), r3(dkb), dvt, w["lam_params"], lam_init, tq=1024)
        hp = _out_ffn(hp, gla_o.reshape(B * S, GV_W), gg, diff_o.reshape(B * S, DV_W),
                      p_prompt[i].reshape(B * S, -1), w, lam_init, tm=512)
        kp_l.append(jnp.transpose(dkt.reshape(B, DIFF_HEADS, 2, DIFF_DH, S), (0, 4, 1, 2, 3)))
        vp_l.append(dv4.reshape(B, S, DIFF_HEADS, DIFF_DV))
        gp_l.append(gla_s.reshape(B, GLA_HEADS, GLA_DK, GLA_DV))

        gq, gk, gv, gg, la, dq, dk, dv = _proj(hs, cos_s, sin_s, w, tm=DB * T, pos_period_blocks=1)
        r3 = lambda a: a.reshape(DB, T, a.shape[-1])
        padt = lambda a: jnp.pad(r3(a), ((0, 0), (0, t_pad - T), (0, 0)))
        s0 = state_gla[i].reshape(DB, GK_W, GLA_DV)
        gla_o, gla_s = _gla(padt(gq), padt(gk), padt(gv), padt(la), s0, block=t_pad)
        gla_o = gla_o[:, :T]
        diff_o = _attn_sample(r3(dq), r3(dk), r3(dv), cache_k2, cache_v2, page_table + i * n_pool,
                              w["lam_params"], lam_init, pages=16)
        hs = _out_ffn(hs, gla_o.reshape(DB * T, GV_W), gg, diff_o.reshape(DB * T, DV_W),
                      p_sample[i].reshape(DB * T, -1), w, lam_init, tm=DB * T)
        ks_l.append(dk.reshape(DB, T, DIFF_HEADS, 2, DIFF_DH))
        vs_l.append(dv.reshape(DB, T, DIFF_HEADS, DIFF_DV))
        gs_l.append(gla_s.reshape(DB, GLA_HEADS, GLA_DK, GLA_DV))

    return (hp.reshape(B, S, D_MODEL), hs.reshape(DB, T, D_MODEL),
            jnp.stack(kp_l), jnp.stack(vp_l), jnp.stack(gp_l),
            jnp.stack(ks_l), jnp.stack(vs_l), jnp.stack(gs_l))
```

```python
import functools
import math

import jax
import jax.numpy as jnp
import numpy as np
from jax import lax
from jax.experimental import pallas as pl
from jax.experimental.pallas import tpu as pltpu

F32 = jnp.float32
BF16 = jnp.bfloat16

D_MODEL = 1024
GLA_HEADS = 4
GLA_DV = 128
GLA_DK = 64
GLA_GATE_RANK = 16
GLA_GATE_NORMALIZER = 16.0
GLA_CHUNK = 64
DIFF_HEADS = 4
DIFF_DV = 128
DIFF_DH = 64
ROPE_THETA = 10000.0
PAGE_SIZE = 128
EPS = 1e-6
GK_W = GLA_HEADS * GLA_DK
GV_W = GLA_HEADS * GLA_DV
DQ_W = DIFF_HEADS * 2 * DIFF_DH
DV_W = DIFF_HEADS * DIFF_DV
LANES = 128
LR_PAD = LANES
PROJ_W = 2 * GK_W + 2 * GV_W + 2 * DQ_W + DV_W
VMEM_LIMIT = 56 * 1024 * 1024


def _dot(a, b):
    return jnp.dot(a, b, preferred_element_type=F32)


def _dot_nt(a, b):
    return lax.dot_general(a, b, (((1,), (1,)), ((), ())), preferred_element_type=F32)


def _dot_tn(a, b):
    return lax.dot_general(a, b, (((0,), (0,)), ((), ())), preferred_element_type=F32)


def _rms(x, g):
    return x * lax.rsqrt(jnp.mean(x * x, axis=-1, keepdims=True) + EPS) * g


def _const_spec(shape):
    nd = len(shape)
    return pl.BlockSpec(shape, lambda *_: (0,) * nd, pipeline_mode=pl.Buffered(1))


def _proj_kernel(x_ref, ng_ref, w_ref, wlr_ref, wgk2_ref, bgk_ref, qg_ref, kg_ref, cos_ref,
                 sin_ref, ind_ref, gq_ref, gk_ref, gv_ref, gg_ref, la_ref, dq_ref, *kv_refs,
                 feature_major):
    x = x_ref[...]
    xn = _rms(x, ng_ref[...]).astype(BF16)
    qk = _dot(xn, w_ref[:, :2 * DQ_W])
    dq = qk[:, :DQ_W]
    dk = qk[:, DQ_W:]

    tm = x.shape[0]
    cos = jnp.concatenate([cos_ref[...]] * (DQ_W // LANES), axis=1)
    sin = jnp.concatenate([sin_ref[...]] * (DQ_W // LANES), axis=1)
    lane = lax.broadcasted_iota(jnp.int32, (tm, DQ_W), 1)
    upper = (lane & (DIFF_DH // 2)) != 0
    ind = ind_ref[...]

    def norm_rope(y, g):
        ms = jnp.concatenate(
            [_dot((y[:, t * LANES:(t + 1) * LANES] ** 2).astype(BF16), ind)
             for t in range(DQ_W // LANES)], axis=1) * (1.0 / DIFF_DH)
        yn = y * lax.rsqrt(ms + EPS) * g
        partner = jnp.where(upper, pltpu.roll(yn, DIFF_DH // 2, 1),
                            pltpu.roll(yn, DQ_W - DIFF_DH // 2, 1))
        return yn * cos + partner * sin

    dq_ref[...] = norm_rope(dq, qg_ref[...])
    dk = norm_rope(dk, kg_ref[...])
    if feature_major:
        dkb_ref, dkt_ref, dv4_ref, dvt_ref = kv_refs
        dkb_ref[...] = dk.astype(BF16)
        dkt_ref[...] = dk.T
    else:
        dk_ref, dv_ref = kv_refs
        dk_ref[...] = dk

    glr = _dot(xn, wlr_ref[...])
    z = _dot(glr.astype(BF16), wgk2_ref[...]) + bgk_ref[...]
    logsig = jnp.minimum(z, 0.0) - jnp.log(1.0 + jnp.exp(-jnp.abs(z)))
    la_ref[...] = logsig * (1.0 / GLA_GATE_NORMALIZER)

    rest = _dot(xn, w_ref[:, 2 * DQ_W:])
    o = 0
    gq_ref[...] = rest[:, o:o + GK_W] * (GLA_DK ** -0.5); o += GK_W
    gk_ref[...] = rest[:, o:o + GK_W]; o += GK_W
    gv_ref[...] = rest[:, o:o + GV_W]; o += GV_W
    gg_ref[...] = rest[:, o:o + GV_W]; o += GV_W
    dv = rest[:, o:o + DV_W]
    if feature_major:
        for h in range(DIFF_HEADS):
            dv4_ref[pl.ds(h, tm, stride=DIFF_HEADS), :] = dv[:, h * DIFF_DV:(h + 1) * DIFF_DV]
        dvt_ref[...] = dv.T.astype(BF16)
    else:
        dv_ref[...] = dv


def _proj(x, pos_cos, pos_sin, w, *, tm, pos_period_blocks, seq=None):
    n = x.shape[0]
    grid = (n // tm,)
    tok = lambda width: pl.BlockSpec((tm, width), lambda i: (i, 0))
    pos = pl.BlockSpec((tm, LANES), lambda i: (i % pos_period_blocks, 0))
    out_widths = (GK_W, GK_W, GV_W, GV_W, GK_W, DQ_W)
    out_specs = [tok(wd) for wd in out_widths]
    out_shape = [jax.ShapeDtypeStruct((n, wd), F32) for wd in out_widths]
    if seq is None:
        out_specs += [tok(DQ_W), tok(DV_W)]
        out_shape += [jax.ShapeDtypeStruct((n, DQ_W), F32), jax.ShapeDtypeStruct((n, DV_W), F32)]
    else:
        per_seq = seq // tm
        fm = lambda width: pl.BlockSpec((None, width, tm), lambda i: (i // per_seq, 0, i % per_seq))
        out_specs += [tok(DQ_W), fm(DQ_W),
                      pl.BlockSpec((tm * DIFF_HEADS, DIFF_DV), lambda i: (i, 0)), fm(DV_W)]
        out_shape += [jax.ShapeDtypeStruct((n, DQ_W), BF16),
                      jax.ShapeDtypeStruct((n // seq, DQ_W, seq), F32),
                      jax.ShapeDtypeStruct((n * DIFF_HEADS, DIFF_DV), F32),
                      jax.ShapeDtypeStruct((n // seq, DV_W, seq), BF16)]
    return pl.pallas_call(
        functools.partial(_proj_kernel, feature_major=seq is not None),
        grid=grid,
        in_specs=[tok(D_MODEL), _const_spec((1, D_MODEL)), _const_spec((D_MODEL, PROJ_W)),
                  _const_spec((D_MODEL, LR_PAD)), _const_spec((LR_PAD, GK_W)),
                  _const_spec((1, GK_W)), _const_spec((1, DQ_W)), _const_spec((1, DQ_W)),
                  pos, pos, _const_spec((LANES, LANES))],
        out_specs=out_specs,
        out_shape=out_shape,
        compiler_params=pltpu.CompilerParams(dimension_semantics=("arbitrary",),
                                             vmem_limit_bytes=VMEM_LIMIT),
        name="proj",
    )(x, w["attn_norm_g"], w["w_proj"], w["w_lr"], w["w_gk2"], w["b_gk"], w["q_norm_g"],
      w["k_norm_g"], pos_cos, pos_sin, w["group_ind"])


GLA_GROUP = 128
GLA_LEVELS = GLA_CHUNK.bit_length() - 1


def _gla_constants():
    n = GLA_GROUP
    t = np.arange(n)[:, None]
    u = np.arange(n)[None, :]
    mats = [((t // GLA_CHUNK == u // GLA_CHUNK) & (u <= t)).astype(np.float32)]
    masks = []
    for lvl in range(GLA_LEVELS):
        h = 1 << lvl
        ref = (t // (2 * h)) * (2 * h) + h - 1
        upper = (t & h) != 0
        mats.append(np.where(upper & (u > ref) & (u <= t), 1.0, 0.0)
                    - np.where(~upper & (u > t) & (u <= ref), 1.0, 0.0))
        masks.append(((t // (2 * h) == u // (2 * h)) & upper & ((u & h) == 0)).astype(np.float32))
    head = np.arange(GK_W)[:, None] // GLA_DK == np.arange(GV_W)[None, :] // GLA_DV
    return (jnp.asarray(np.concatenate(mats, 0), BF16), jnp.asarray(np.concatenate(masks, 0), BF16),
            jnp.asarray(head.astype(np.float32), BF16))


def _gla_kernel(q_ref, k_ref, v_ref, la_ref, s0_ref, mexp_ref, mask_ref, expand_ref, o_ref,
                sout_ref, s_scr, ex_scr, a_scr, u_scr, d_scr, sb_scr, *, n_groups):
    j = pl.program_id(1)
    n = GLA_GROUP
    pair_w = 2 * GLA_DK
    n_pairs = GLA_HEADS // 2
    per_group = n // GLA_CHUNK

    @pl.when(j == 0)
    def _():
        s_scr[...] = s0_ref[...]

    def head_only(x, h):
        lane = lax.broadcasted_iota(jnp.int32, x.shape, 1)
        keep = lane < GLA_DK if h % 2 == 0 else lane >= GLA_DK
        return jnp.where(keep, x, 0.0)

    grp = lambda g: slice(g * n, (g + 1) * n)
    chunk_rows = lambda c: slice(c * GLA_CHUNK, (c + 1) * GLA_CHUNK)

    def chunk_b(c):
        lo = (c % per_group) * GLA_CHUNK
        return ex_scr[c // per_group, lo:lo + GLA_CHUNK, :]

    for g in range(n_groups):
        la = la_ref[grp(g), :]
        la_hi = la.astype(BF16)
        la_lo = (la - la_hi.astype(F32)).astype(BF16)
        ex_scr[g, :n, :] = _dot(mexp_ref[:n, :], la_hi) + _dot(mexp_ref[:n, :], la_lo)
        ex_scr[g, n:, :] = _dot(mexp_ref[n:, :], la_hi)

    for g in range(n_groups):
        q = q_ref[grp(g), :]
        k = k_ref[grp(g), :]
        a = [None] * GLA_HEADS
        for lvl in range(GLA_LEVELS):
            d = ex_scr[g, (lvl + 1) * n:(lvl + 2) * n, :]
            ql = q * jnp.exp(jnp.minimum(d, 0.0))
            kl = (k * jnp.exp(jnp.minimum(-d, 0.0))).astype(BF16)
            msk = mask_ref[lvl * n:(lvl + 1) * n, :] > 0
            for h in range(GLA_HEADS):
                tile = slice((h // 2) * pair_w, (h // 2 + 1) * pair_w)
                t = _dot_nt(head_only(ql[:, tile], h).astype(BF16), kl[:, tile])
                t = jnp.where(msk, t, 0.0)
                a[h] = t if a[h] is None else a[h] + t
        for h in range(GLA_HEADS):
            a_scr[g, h] = a[h].astype(BF16)

    for g in range(n_groups):
        v = v_ref[grp(g), :]
        vb = v.astype(BF16)
        o = _dot((q_ref[grp(g), :] * k_ref[grp(g), :]).astype(BF16), expand_ref[...]) * v
        o_ref[grp(g), :] = o + jnp.concatenate(
            [_dot(a_scr[g, h], vb[:, h * GLA_DV:(h + 1) * GLA_DV]) for h in range(GLA_HEADS)],
            axis=1)

    for c in range(n_groups * per_group):
        r = chunk_rows(c)
        bc = chunk_b(c)
        b_last = bc[GLA_CHUNK - 1:GLA_CHUNK, :]
        ke = (k_ref[r, :] * jnp.exp(b_last - bc)).astype(BF16)
        vb = v_ref[r, :].astype(BF16)
        decay = jnp.exp(jnp.broadcast_to(b_last, (GLA_DV, GK_W))).T
        for pair in range(n_pairs):
            tile = slice(pair * pair_w, (pair + 1) * pair_w)
            upd = []
            for h in (2 * pair, 2 * pair + 1):
                kv = _dot_tn(ke[:, tile], vb[:, h * GLA_DV:(h + 1) * GLA_DV])
                upd.append(kv[(h % 2) * GLA_DK:(h % 2 + 1) * GLA_DK])
            u_scr[c, pair] = jnp.concatenate(upd, axis=0)
            d_scr[c, pair] = decay[tile, :]

    for c in range(n_groups * per_group):
        for pair in range(n_pairs):
            tile = slice(pair * pair_w, (pair + 1) * pair_w)
            s_pair = s_scr[tile, :]
            sb_scr[c, pair] = s_pair.astype(BF16)
            s_scr[tile, :] = d_scr[c, pair] * s_pair + u_scr[c, pair]

    for c in range(n_groups * per_group):
        r = chunk_rows(c)
        qe = q_ref[r, :] * jnp.exp(chunk_b(c))
        heads = []
        for h in range(GLA_HEADS):
            tile = slice((h // 2) * pair_w, (h // 2 + 1) * pair_w)
            heads.append(_dot(head_only(qe[:, tile], h).astype(BF16), sb_scr[c, h // 2]))
        o_ref[r, :] += jnp.concatenate(heads, axis=1)

    @pl.when(j == pl.num_programs(1) - 1)
    def _():
        sout_ref[...] = s_scr[...]


def _gla(q, k, v, la, s0, *, block):
    b, t, _ = q.shape
    assert t % block == 0 and block % GLA_GROUP == 0
    mexp, masks, expand = _gla_constants()
    n_groups = block // GLA_GROUP
    n_chunks = block // GLA_CHUNK
    n_pairs = GLA_HEADS // 2
    grid = (b, t // block)
    tok = lambda width: pl.BlockSpec((None, block, width), lambda i, j: (i, j, 0))
    st = pl.BlockSpec((None, GK_W, GLA_DV), lambda i, j: (i, 0, 0))
    return pl.pallas_call(
        functools.partial(_gla_kernel, n_groups=n_groups),
        grid=grid,
        in_specs=[tok(GK_W), tok(GK_W), tok(GV_W), tok(GK_W), st, _const_spec(mexp.shape),
                  _const_spec(masks.shape), _const_spec(expand.shape)],
        out_specs=[tok(GV_W), st],
        out_shape=[jax.ShapeDtypeStruct((b, t, GV_W), F32),
                   jax.ShapeDtypeStruct((b, GK_W, GLA_DV), F32)],
        scratch_shapes=[pltpu.VMEM((GK_W, GLA_DV), F32),
                        pltpu.VMEM((n_groups, (GLA_LEVELS + 1) * GLA_GROUP, GK_W), F32),
                        pltpu.VMEM((n_groups, GLA_HEADS, GLA_GROUP, GLA_GROUP), BF16),
                        pltpu.VMEM((n_chunks, n_pairs, 2 * GLA_DK, GLA_DV), F32),
                        pltpu.VMEM((n_chunks, n_pairs, 2 * GLA_DK, GLA_DV), F32),
                        pltpu.VMEM((n_chunks, n_pairs, 2 * GLA_DK, GLA_DV), BF16)],
        compiler_params=pltpu.CompilerParams(dimension_semantics=("arbitrary", "arbitrary"),
                                             vmem_limit_bytes=VMEM_LIMIT),
        name="gla",
    )(q, k, v, la, s0, mexp, masks, expand)


def _lambda(lp_ref, lam_init):
    lp = lp_ref[...]
    s1 = jnp.sum(lp[0:1, :] * lp[1:2, :], axis=1, keepdims=True)
    s2 = jnp.sum(lp[2:3, :] * lp[3:4, :], axis=1, keepdims=True)
    return jnp.exp(s1) - jnp.exp(s2) + lam_init


_RELAYOUT_ROWS = 512
ATTN_CHAIN = 256
ATTN_SLOTS = 32
ONES_ROWS = 16


def _attn_prompt_kernel(lp_ref, q_ref, kb_ref, vt_ref, o_ref, vt_scr, qst_scr, st_scr, m_scr,
                        acc_scr, *, tq, seq, lam_init):
    i = pl.program_id(2)
    cw = ATTN_CHAIN
    per_sub = tq // cw
    n_chains = 2 * per_sub

    @pl.when(i == 0)
    def _():
        vt_scr[:DIFF_DV, :] = vt_ref[...]
        vt_scr[DIFF_DV:, :] = jnp.ones((ONES_ROWS, seq), BF16)

    feat = lax.broadcasted_iota(jnp.int32, (2 * DIFF_DH, _RELAYOUT_ROWS), 0)
    first = feat < DIFF_DH
    for c in range(tq // _RELAYOUT_ROWS):
        r = slice(c * _RELAYOUT_ROWS, (c + 1) * _RELAYOUT_ROWS)
        qt = (q_ref[r, :] * (DIFF_DH ** -0.5 * math.log2(math.e))).T
        qst_scr[:, r] = jnp.where(first, qt, 0.0).astype(BF16)
        qst_scr[:, tq + c * _RELAYOUT_ROWS:tq + (c + 1) * _RELAYOUT_ROWS] = (
            jnp.where(first, 0.0, qt).astype(BF16))

    m_scr[...] = jnp.full(m_scr.shape, -jnp.inf, F32)
    acc_scr[...] = jnp.zeros(acc_scr.shape, F32)

    kpos = lax.broadcasted_iota(jnp.int32, (cw, cw), 0)
    qpos = lax.broadcasted_iota(jnp.int32, (cw, cw), 1)
    causal = kpos <= qpos

    def scores(slot, r, c):
        st_scr[:, slot * cw:(slot + 1) * cw] = _dot(kb_ref[r, :], qst_scr[:, c * cw:(c + 1) * cw])

    def softmax_pv(slot, r, c, masked):
        cols = slice(c * cw, (c + 1) * cw)
        st = st_scr[:, slot * cw:(slot + 1) * cw]
        if masked:
            st = jnp.where(causal, st, -jnp.inf)
        m_old = m_scr[:, cols]
        m_new = jnp.maximum(m_old, jnp.max(st, axis=0, keepdims=True))
        p = jnp.exp2(st - m_new).astype(BF16)
        acc_scr[:, cols] = jnp.exp2(m_old - m_new) * acc_scr[:, cols] + _dot(vt_scr[:, r], p)
        m_scr[:, cols] = m_new

    def run(items):
        assert len(items) <= ATTN_SLOTS
        for slot, (r, c, _) in enumerate(items):
            scores(slot, r, c)
        for slot, (r, c, masked) in enumerate(items):
            softmax_pv(slot, r, c, masked)

    blocks_per_trip = ATTN_SLOTS // n_chains

    def body(j, carry):
        items = []
        for u in range(blocks_per_trip):
            r = pl.ds(pl.multiple_of((j * blocks_per_trip + u) * cw, cw), cw)
            items += [(r, c, False) for c in range(n_chains)]
        run(items)
        return carry

    lax.fori_loop(0, i * per_sub // blocks_per_trip, body, 0)

    items = []
    for jj in range(per_sub):
        r = pl.ds(pl.multiple_of(i * tq + jj * cw, cw), cw)
        block = [(r, c, c % per_sub == jj) for c in range(n_chains) if c % per_sub >= jj]
        if len(items) + len(block) > ATTN_SLOTS:
            run(items)
            items = []
        items += block
    run(items)

    lam = _lambda(lp_ref, lam_init)
    for c in range(per_sub):
        c1 = slice(c * cw, (c + 1) * cw)
        c2 = slice(tq + c * cw, tq + (c + 1) * cw)
        o1 = acc_scr[:DIFF_DV, c1] / acc_scr[DIFF_DV:DIFF_DV + 1, c1]
        o2 = acc_scr[:DIFF_DV, c2] / acc_scr[DIFF_DV:DIFF_DV + 1, c2]
        o_ref[c1, :] = (o1 - lam * o2).T


def _attn_prompt(q, kb, vt, lam_params, lam_init, *, tq):
    b, s, _ = q.shape
    per_sub = tq // ATTN_CHAIN
    assert tq % ATTN_CHAIN == 0 and ATTN_SLOTS % (2 * per_sub) == 0
    assert per_sub % (ATTN_SLOTS // (2 * per_sub)) == 0
    grid = (b, DIFF_HEADS, s // tq)
    width = 2 * DIFF_DH
    qspec = pl.BlockSpec((None, tq, width), lambda bi, h, i: (bi, i, h))
    kspec = pl.BlockSpec((None, s, width), lambda bi, h, i: (bi, 0, h))
    vtspec = pl.BlockSpec((None, DIFF_DV, s), lambda bi, h, i: (bi, h, 0))
    return pl.pallas_call(
        functools.partial(_attn_prompt_kernel, tq=tq, seq=s, lam_init=lam_init),
        grid=grid,
        in_specs=[_const_spec((4, DIFF_DH)), qspec, kspec, vtspec],
        out_specs=qspec,
        out_shape=jax.ShapeDtypeStruct((b, s, DV_W), F32),
        scratch_shapes=[pltpu.VMEM((DIFF_DV + ONES_ROWS, s), BF16),
                        pltpu.VMEM((width, 2 * tq), BF16),
                        pltpu.VMEM((ATTN_CHAIN, ATTN_SLOTS * ATTN_CHAIN), F32),
                        pltpu.VMEM((1, 2 * tq), F32),
                        pltpu.VMEM((DIFF_DV + ONES_ROWS, 2 * tq), F32)],
        compiler_params=pltpu.CompilerParams(
            dimension_semantics=("arbitrary", "arbitrary", "arbitrary"),
            vmem_limit_bytes=VMEM_LIMIT),
        name="attn_prompt",
    )(lam_params, q, kb, vt)


def _head_norm(o, g):
    return jnp.concatenate(
        [_rms(o[:, h * LANES:(h + 1) * LANES], g) for h in range(o.shape[1] // LANES)], axis=1)


def _ffn_mix(x_ref, go_ref, gg_ref, do_ref, gn_ref, dn_ref, wout_ref, lam_init):
    gg = gg_ref[...]
    go = _head_norm(go_ref[...], gn_ref[...]) * (gg * jax.nn.sigmoid(gg))
    do = _head_norm(do_ref[...], dn_ref[...]) * (1.0 - lam_init)
    mix = jnp.concatenate([go, do], axis=1).astype(BF16)
    return x_ref[...] + _dot(mix, wout_ref[...])


def _ffn_tail(h, p_ref, fn_ref, wg_ref, wu_ref, wd_ref, pn_ref, wpg_ref, wpp_ref):
    hn = _rms(h, fn_ref[...]).astype(BF16)
    gate = _dot(hn, wg_ref[...])
    f = (gate * jax.nn.sigmoid(gate)) * _dot(hn, wu_ref[...])
    h = h + _dot(f.astype(BF16), wd_ref[...])
    pg = jax.nn.sigmoid(_dot(_rms(h, pn_ref[...]).astype(BF16), wpg_ref[...]))
    return h + pg * _dot(p_ref[...].astype(BF16), wpp_ref[...])


N_FFN_INPUTS = 15


def _attn_sample_kernel(pt_ref, lp_ref, q_ref, kn_ref, vn_ref, *refs, pages, t_new, lam_init,
                        with_ffn):
    k_refs = refs[:pages]
    v_refs = refs[pages:2 * pages]
    refs = refs[2 * pages:]
    if with_ffn:
        (x_ref, go_ref, gg_ref, do_ref, p_ref, gn_ref, dn_ref, wout_ref, fn_ref, wg_ref, wu_ref,
         wd_ref, pn_ref, wpg_ref, wpp_ref) = refs[:N_FFN_INPUTS]
        o_ref, y_ref, qs_scr, m_scr, l_scr, acc_scr = refs[N_FFN_INPUTS:]
    else:
        o_ref, qs_scr, m_scr, l_scr, acc_scr = refs
    p = pl.program_id(1)
    n_rows = 2 * DIFF_HEADS * t_new
    head_rows = 2 * t_new

    @pl.when(p == 0)
    def _():
        q = q_ref[...] * (DIFF_DH ** -0.5)
        lane = lax.broadcasted_iota(jnp.int32, (t_new, DQ_W), 1) // DIFF_DH
        qs = jnp.concatenate([jnp.where(lane == g, q, 0.0) for g in range(2 * DIFF_HEADS)],
                             axis=0).astype(BF16)
        qs_scr[...] = qs
        s = _dot_nt(qs, kn_ref[...].astype(BF16))
        qpos = lax.broadcasted_iota(jnp.int32, (n_rows, t_new), 0) % t_new
        kpos = lax.broadcasted_iota(jnp.int32, (n_rows, t_new), 1)
        s = jnp.where(kpos <= qpos, s, -jnp.inf)
        m = jnp.max(s, axis=1, keepdims=True)
        e = jnp.exp(s - m)
        m_scr[...] = m
        l_scr[...] = jnp.sum(e, axis=1, keepdims=True)
        eb = e.astype(BF16)
        vn = vn_ref[...].astype(BF16)
        acc_scr[...] = jnp.concatenate(
            [_dot(eb[h * head_rows:(h + 1) * head_rows, :], vn[:, h * DIFF_DV:(h + 1) * DIFF_DV])
             for h in range(DIFF_HEADS)], axis=0)

    qs = qs_scr[...]
    s = jnp.concatenate(
        [_dot(qs, jnp.concatenate([k_refs[g][...], k_refs[g + 1][...]], axis=1).astype(BF16))
         for g in range(0, pages, 2)], axis=1)
    if with_ffn:
        h_mid = _ffn_mix(x_ref, go_ref, gg_ref, do_ref, gn_ref, dn_ref, wout_ref, lam_init)
    m_old = m_scr[...]
    m_new = jnp.maximum(m_old, jnp.max(s, axis=1, keepdims=True))
    alpha = jnp.exp(m_old - m_new)
    e = jnp.exp(s - m_new)
    l_scr[...] = alpha * l_scr[...] + jnp.sum(e, axis=1, keepdims=True)
    eb = e.astype(BF16)
    pv = []
    for h in range(DIFF_HEADS):
        rows = slice(h * head_rows, (h + 1) * head_rows)
        acc_h = None
        for g in range(0, pages, 2):
            v_h = jnp.concatenate(
                [v_refs[g][pl.ds(h, PAGE_SIZE, stride=DIFF_HEADS), :],
                 v_refs[g + 1][pl.ds(h, PAGE_SIZE, stride=DIFF_HEADS), :]], axis=0).astype(BF16)
            t = _dot(eb[rows, g * PAGE_SIZE:(g + 2) * PAGE_SIZE], v_h)
            acc_h = t if acc_h is None else acc_h + t
        pv.append(acc_h)
    if with_ffn:
        y_ref[...] = _ffn_tail(h_mid, p_ref, fn_ref, wg_ref, wu_ref, wd_ref, pn_ref, wpg_ref, wpp_ref)
    acc_scr[...] = alpha * acc_scr[...] + jnp.concatenate(pv, axis=0)
    m_scr[...] = m_new

    @pl.when(p == pl.num_programs(1) - 1)
    def _():
        out = acc_scr[...] / l_scr[...]
        lam = _lambda(lp_ref, lam_init)
        heads = []
        for h in range(DIFF_HEADS):
            r1 = h * head_rows
            r2 = r1 + t_new
            heads.append(out[r1:r1 + t_new, :] - lam * out[r2:r2 + t_new, :])
        o_ref[...] = jnp.concatenate(heads, axis=1)


def _ffn_operands(x, go, gg, do, p, w):
    return (x, go, gg, do, p, w["gla_norm_g"], w["diff_norm_g"], w["w_out"], w["ffn_norm_g"],
            w["w_ffn_gate"], w["w_ffn_up"], w["w_ffn_down"], w["ple_norm_g"], w["w_ple_gate"],
            w["w_ple_proj"])


def _ffn_in_specs(tok, ple, d_ff):
    return [tok(D_MODEL), tok(GV_W), tok(GV_W), tok(DV_W), tok(ple),
            _const_spec((1, GLA_DV)), _const_spec((1, DIFF_DV)),
            _const_spec((D_MODEL, D_MODEL)), _const_spec((1, D_MODEL)),
            _const_spec((D_MODEL, d_ff)), _const_spec((D_MODEL, d_ff)),
            _const_spec((d_ff, D_MODEL)), _const_spec((1, D_MODEL)),
            _const_spec((D_MODEL, D_MODEL)), _const_spec((ple, D_MODEL))]


def _attn_sample(q, k_new, v_new, cache_kt, cache_v, page_table, lam_params, lam_init, *, pages,
                 ffn=None):
    db, t_new, _ = q.shape
    n_pages = page_table.shape[1]
    per_row = n_pages // pages
    grid = (db, per_row)
    n_rows = 2 * DIFF_HEADS * t_new
    tok = pl.BlockSpec((None, t_new, DQ_W), lambda b, p, pt: (b, 0, 0))

    def page_spec(g):
        return pl.BlockSpec((None, DQ_W, PAGE_SIZE), lambda b, p, pt: (pt[b, p * pages + g], 0, 0))

    in_specs = ([pl.BlockSpec((4, DIFF_DH), lambda b, p, pt: (0, 0)), tok, tok, tok]
                + [page_spec(g) for g in range(pages)] + [page_spec(g) for g in range(pages)])
    operands = [page_table, lam_params, q, k_new, v_new, *([cache_kt] * pages), *([cache_v] * pages)]
    out_specs = [tok]
    out_shape = [jax.ShapeDtypeStruct((db, t_new, DV_W), F32)]
    if ffn is not None:
        x, w = ffn[0], ffn[5]
        n = x.shape[0]
        tm = n // (db * per_row)
        assert tm * db * per_row == n and tm % 8 == 0
        ftok = lambda width: pl.BlockSpec((tm, width), lambda b, p, pt: (b * per_row + p, 0))
        in_specs += _ffn_in_specs(ftok, ffn[4].shape[1], w["w_ffn_gate"].shape[1])
        operands += list(_ffn_operands(*ffn))
        out_specs.append(ftok(D_MODEL))
        out_shape.append(jax.ShapeDtypeStruct((n, D_MODEL), F32))
    grid_spec = pltpu.PrefetchScalarGridSpec(
        num_scalar_prefetch=1,
        grid=grid,
        in_specs=in_specs,
        out_specs=out_specs,
        scratch_shapes=[pltpu.VMEM((n_rows, DQ_W), BF16), pltpu.VMEM((n_rows, 1), F32),
                        pltpu.VMEM((n_rows, 1), F32), pltpu.VMEM((n_rows, DIFF_DV), F32)],
    )
    res = pl.pallas_call(
        functools.partial(_attn_sample_kernel, pages=pages, t_new=t_new, lam_init=lam_init,
                          with_ffn=ffn is not None),
        grid_spec=grid_spec,
        out_shape=out_shape,
        compiler_params=pltpu.CompilerParams(dimension_semantics=("arbitrary", "arbitrary"),
                                             vmem_limit_bytes=VMEM_LIMIT),
        name="attn_sample",
    )(*operands)
    return res if ffn is not None else res[0]


def _out_ffn_kernel(x_ref, go_ref, gg_ref, do_ref, p_ref, gn_ref, dn_ref, wout_ref, fn_ref,
                    wg_ref, wu_ref, wd_ref, pn_ref, wpg_ref, wpp_ref, y_ref, *, lam_init):
    h = _ffn_mix(x_ref, go_ref, gg_ref, do_ref, gn_ref, dn_ref, wout_ref, lam_init)
    y_ref[...] = _ffn_tail(h, p_ref, fn_ref, wg_ref, wu_ref, wd_ref, pn_ref, wpg_ref, wpp_ref)


def _out_ffn(x, go, gg, do, p, w, lam_init, *, tm):
    n = x.shape[0]
    tok = lambda width: pl.BlockSpec((tm, width), lambda i: (i, 0))
    return pl.pallas_call(
        functools.partial(_out_ffn_kernel, lam_init=lam_init),
        grid=(n // tm,),
        in_specs=_ffn_in_specs(tok, p.shape[1], w["w_ffn_gate"].shape[1]),
        out_specs=tok(D_MODEL),
        out_shape=jax.ShapeDtypeStruct((n, D_MODEL), F32),
        compiler_params=pltpu.CompilerParams(dimension_semantics=("arbitrary",),
                                             vmem_limit_bytes=VMEM_LIMIT),
        name="out_ffn",
    )(*_ffn_operands(x, go, gg, do, p, w))


def _rope_tables(pos):
    half = DIFF_DH // 2
    inv = ROPE_THETA ** (-jnp.arange(half, dtype=F32) / half)
    ang = pos.astype(F32)[:, None] * inv[None, :]
    cos, sin = jnp.cos(ang), jnp.sin(ang)
    cos = jnp.concatenate([cos, cos], axis=1)
    sin = jnp.concatenate([-sin, sin], axis=1)
    reps = LANES // DIFF_DH
    return jnp.tile(cos, (1, reps)), jnp.tile(sin, (1, reps))


def _layer_weights(i, attn_norm_g, w_in, w_gk2, b_gk, q_norm_g, k_norm_g, lam_params, gla_norm_g,
                   diff_norm_g, w_out, ffn_norm_g, w_ffn_gate, w_ffn_up, w_ffn_down, ple_norm_g,
                   w_ple_gate, w_ple_proj):
    lr0 = 2 * GK_W + 2 * GV_W
    wi = w_in[i]
    w_lr = jnp.pad(wi[:, lr0:lr0 + GLA_GATE_RANK], ((0, 0), (0, LR_PAD - GLA_GATE_RANK)))
    gid = jnp.arange(LANES) // DIFF_DH
    dq0 = lr0 + GLA_GATE_RANK
    row = lambda a: a.reshape(1, -1).astype(F32)
    return {
        "attn_norm_g": row(attn_norm_g[i]),
        "w_proj": jnp.concatenate([wi[:, dq0:dq0 + 2 * DQ_W], wi[:, :lr0], wi[:, dq0 + 2 * DQ_W:]],
                                  axis=1).astype(BF16),
        "w_lr": w_lr.astype(BF16),
        "w_gk2": jnp.pad(w_gk2[i], ((0, LR_PAD - GLA_GATE_RANK), (0, 0))).astype(BF16),
        "b_gk": row(b_gk[i]),
        "q_norm_g": row(jnp.tile(q_norm_g[i], DQ_W // DIFF_DH)),
        "k_norm_g": row(jnp.tile(k_norm_g[i], DQ_W // DIFF_DH)),
        "group_ind": (gid[:, None] == gid[None, :]).astype(BF16),
        "lam_params": lam_params[i].astype(F32),
        "gla_norm_g": row(gla_norm_g[i]),
        "diff_norm_g": row(diff_norm_g[i]),
        "w_out": w_out[i].astype(BF16),
        "ffn_norm_g": row(ffn_norm_g[i]),
        "w_ffn_gate": w_ffn_gate[i].astype(BF16),
        "w_ffn_up": w_ffn_up[i].astype(BF16),
        "w_ffn_down": w_ffn_down[i].astype(BF16),
        "ple_norm_g": row(ple_norm_g[i]),
        "w_ple_gate": w_ple_gate[i].astype(BF16),
        "w_ple_proj": w_ple_proj[i].astype(BF16),
    }


def kernel(x_prompt, x_sample, p_prompt, p_sample, cache_k, cache_v, state_gla, page_table, attn_norm_g, w_in, w_gk2, b_gk, q_norm_g, k_norm_g, lam_params, gla_norm_g, diff_norm_g, w_out, ffn_norm_g, w_ffn_gate, w_ffn_up, w_ffn_down, ple_norm_g, w_ple_gate, w_ple_proj):
    B, S, _ = x_prompt.shape
    DB, T, _ = x_sample.shape
    depth = w_in.shape[0]
    n_pool = cache_k.shape[1]
    n_pages = page_table.shape[1]
    past_len = n_pages * PAGE_SIZE

    tm_p = 512
    cos_p, sin_p = _rope_tables(jnp.arange(S))
    cos_s, sin_s = _rope_tables(jnp.tile(past_len + jnp.arange(T), DB))
    t_pad = GLA_GROUP
    cache_k2 = jnp.transpose(cache_k, (0, 1, 3, 4, 5, 2)).reshape(depth * n_pool, DQ_W, PAGE_SIZE)
    cache_v2 = cache_v.reshape(depth * n_pool, PAGE_SIZE * DIFF_HEADS, DIFF_DV)

    hp = x_prompt.reshape(B * S, D_MODEL)
    hs = x_sample.reshape(DB * T, D_MODEL)
    kp_l, vp_l, gp_l, ks_l, vs_l, gs_l = [], [], [], [], [], []
    for i in range(depth):
        lam_init = 0.8 - 0.6 * math.exp(-0.3 * i)
        w = _layer_weights(i, attn_norm_g, w_in, w_gk2, b_gk, q_norm_g, k_norm_g, lam_params,
                           gla_norm_g, diff_norm_g, w_out, ffn_norm_g, w_ffn_gate, w_ffn_up,
                           w_ffn_down, ple_norm_g, w_ple_gate, w_ple_proj)

        gq, gk, gv, gg, la, dq, dkb, dkt, dv4, dvt = _proj(hp, cos_p, sin_p, w, tm=tm_p,
                                                           pos_period_blocks=S // tm_p, seq=S)
        r3 = lambda a: a.reshape(B, S, a.shape[-1])
        s0 = jnp.zeros((B, GK_W, GLA_DV), F32)
        gla_o, gla_s = _gla(r3(gq), r3(gk), r3(gv), r3(la), s0, block=512)
        diff_o = _attn_prompt(r3(dq), r3(dkb), dvt, w["lam_params"], lam_init, tq=1024)
        ffn_prompt = (hp, gla_o.reshape(B * S, GV_W), gg, diff_o.reshape(B * S, DV_W),
                      p_prompt[i].reshape(B * S, -1), w)
        kp_l.append(jnp.transpose(dkt.reshape(B, DIFF_HEADS, 2, DIFF_DH, S), (0, 4, 1, 2, 3)))
        vp_l.append(dv4.reshape(B, S, DIFF_HEADS, DIFF_DV))
        gp_l.append(gla_s.reshape(B, GLA_HEADS, GLA_DK, GLA_DV))

        gq, gk, gv, gg, la, dq, dk, dv = _proj(hs, cos_s, sin_s, w, tm=DB * T, pos_period_blocks=1)
        r3 = lambda a: a.reshape(DB, T, a.shape[-1])
        padt = lambda a: jnp.pad(r3(a), ((0, 0), (0, t_pad - T), (0, 0)))
        s0 = state_gla[i].reshape(DB, GK_W, GLA_DV)
        gla_o, gla_s = _gla(padt(gq), padt(gk), padt(gv), padt(la), s0, block=t_pad)
        gla_o = gla_o[:, :T]
        diff_o, hp = _attn_sample(r3(dq), r3(dk), r3(dv), cache_k2, cache_v2,
                                  page_table + i * n_pool, w["lam_params"], lam_init, pages=16,
                                  ffn=ffn_prompt)
        hs = _out_ffn(hs, gla_o.reshape(DB * T, GV_W), gg, diff_o.reshape(DB * T, DV_W),
                      p_sample[i].reshape(DB * T, -1), w, lam_init, tm=DB * T)
        ks_l.append(dk.reshape(DB, T, DIFF_HEADS, 2, DIFF_DH))
        vs_l.append(dv.reshape(DB, T, DIFF_HEADS, DIFF_DV))
        gs_l.append(gla_s.reshape(DB, GLA_HEADS, GLA_DK, GLA_DV))

    return (hp.reshape(B, S, D_MODEL), hs.reshape(DB, T, D_MODEL),
            jnp.stack(kp_l), jnp.stack(vp_l), jnp.stack(gp_l),
            jnp.stack(ks_l), jnp.stack(vs_l), jnp.stack(gs_l))
```

```python
import functools
import math

import jax
import jax.numpy as jnp
import numpy as np
from jax import lax
from jax.experimental import pallas as pl
from jax.experimental.pallas import tpu as pltpu

F32 = jnp.float32
BF16 = jnp.bfloat16

D_MODEL = 1024
GLA_HEADS = 4
GLA_DV = 128
GLA_DK = 64
GLA_GATE_RANK = 16
GLA_GATE_NORMALIZER = 16.0
GLA_CHUNK = 64
DIFF_HEADS = 4
DIFF_DV = 128
DIFF_DH = 64
ROPE_THETA = 10000.0
PAGE_SIZE = 128
EPS = 1e-6
GK_W = GLA_HEADS * GLA_DK
GV_W = GLA_HEADS * GLA_DV
DQ_W = DIFF_HEADS * 2 * DIFF_DH
DV_W = DIFF_HEADS * DIFF_DV
LANES = 128
LR_PAD = LANES
PROJ_W = 2 * GK_W + 2 * GV_W + 2 * DQ_W + DV_W
VMEM_LIMIT = 56 * 1024 * 1024


def _dot(a, b):
    return jnp.dot(a, b, preferred_element_type=F32)


def _dot_nt(a, b):
    return lax.dot_general(a, b, (((1,), (1,)), ((), ())), preferred_element_type=F32)


def _dot_tn(a, b):
    return lax.dot_general(a, b, (((0,), (0,)), ((), ())), preferred_element_type=F32)


def _rms(x, g):
    return x * lax.rsqrt(jnp.mean(x * x, axis=-1, keepdims=True) + EPS) * g


def _const_spec(shape):
    nd = len(shape)
    return pl.BlockSpec(shape, lambda *_: (0,) * nd, pipeline_mode=pl.Buffered(1))


def _proj_kernel(x_ref, ng_ref, w_ref, wlr_ref, wgk2_ref, bgk_ref, qg_ref, kg_ref, cos_ref,
                 sin_ref, ind_ref, gq_ref, gk_ref, gv_ref, gg_ref, la_ref, dq_ref, *kv_refs,
                 feature_major):
    x = x_ref[...]
    xn = _rms(x, ng_ref[...]).astype(BF16)
    qk = _dot(xn, w_ref[:, :2 * DQ_W])
    dq = qk[:, :DQ_W]
    dk = qk[:, DQ_W:]

    tm = x.shape[0]
    cos = jnp.concatenate([cos_ref[...]] * (DQ_W // LANES), axis=1)
    sin = jnp.concatenate([sin_ref[...]] * (DQ_W // LANES), axis=1)
    lane = lax.broadcasted_iota(jnp.int32, (tm, DQ_W), 1)
    upper = (lane & (DIFF_DH // 2)) != 0
    ind = ind_ref[...]

    def norm_rope(y, g):
        ms = jnp.concatenate(
            [_dot((y[:, t * LANES:(t + 1) * LANES] ** 2).astype(BF16), ind)
             for t in range(DQ_W // LANES)], axis=1) * (1.0 / DIFF_DH)
        yn = y * lax.rsqrt(ms + EPS) * g
        partner = jnp.where(upper, pltpu.roll(yn, DIFF_DH // 2, 1),
                            pltpu.roll(yn, DQ_W - DIFF_DH // 2, 1))
        return yn * cos + partner * sin

    dq_ref[...] = norm_rope(dq, qg_ref[...])
    dk = norm_rope(dk, kg_ref[...])
    if feature_major:
        dkb_ref, dkt_ref, dv4_ref, dvt_ref = kv_refs
        dkb_ref[...] = dk.astype(BF16)
        dkt_ref[...] = dk.T
    else:
        dk_ref, dv_ref = kv_refs
        dk_ref[...] = dk

    glr = _dot(xn, wlr_ref[...])
    z = _dot(glr.astype(BF16), wgk2_ref[...]) + bgk_ref[...]
    logsig = jnp.minimum(z, 0.0) - jnp.log(1.0 + jnp.exp(-jnp.abs(z)))
    la_ref[...] = logsig * (1.0 / GLA_GATE_NORMALIZER)

    rest = _dot(xn, w_ref[:, 2 * DQ_W:])
    o = 0
    gq_ref[...] = rest[:, o:o + GK_W] * (GLA_DK ** -0.5); o += GK_W
    gk_ref[...] = rest[:, o:o + GK_W]; o += GK_W
    gv_ref[...] = rest[:, o:o + GV_W]; o += GV_W
    gg_ref[...] = rest[:, o:o + GV_W]; o += GV_W
    dv = rest[:, o:o + DV_W]
    if feature_major:
        for h in range(DIFF_HEADS):
            dv4_ref[pl.ds(h, tm, stride=DIFF_HEADS), :] = dv[:, h * DIFF_DV:(h + 1) * DIFF_DV]
        dvt_ref[...] = dv.T.astype(BF16)
    else:
        dv_ref[...] = dv


def _proj(x, pos_cos, pos_sin, w, *, tm, pos_period_blocks, seq=None):
    n = x.shape[0]
    grid = (n // tm,)
    tok = lambda width: pl.BlockSpec((tm, width), lambda i: (i, 0))
    pos = pl.BlockSpec((tm, LANES), lambda i: (i % pos_period_blocks, 0))
    out_widths = (GK_W, GK_W, GV_W, GV_W, GK_W, DQ_W)
    out_specs = [tok(wd) for wd in out_widths]
    out_shape = [jax.ShapeDtypeStruct((n, wd), F32) for wd in out_widths]
    if seq is None:
        out_specs += [tok(DQ_W), tok(DV_W)]
        out_shape += [jax.ShapeDtypeStruct((n, DQ_W), F32), jax.ShapeDtypeStruct((n, DV_W), F32)]
    else:
        per_seq = seq // tm
        fm = lambda width: pl.BlockSpec((None, width, tm), lambda i: (i // per_seq, 0, i % per_seq))
        out_specs += [tok(DQ_W), fm(DQ_W),
                      pl.BlockSpec((tm * DIFF_HEADS, DIFF_DV), lambda i: (i, 0)), fm(DV_W)]
        out_shape += [jax.ShapeDtypeStruct((n, DQ_W), BF16),
                      jax.ShapeDtypeStruct((n // seq, DQ_W, seq), F32),
                      jax.ShapeDtypeStruct((n * DIFF_HEADS, DIFF_DV), F32),
                      jax.ShapeDtypeStruct((n // seq, DV_W, seq), BF16)]
    return pl.pallas_call(
        functools.partial(_proj_kernel, feature_major=seq is not None),
        grid=grid,
        in_specs=[tok(D_MODEL), _const_spec((1, D_MODEL)), _const_spec((D_MODEL, PROJ_W)),
                  _const_spec((D_MODEL, LR_PAD)), _const_spec((LR_PAD, GK_W)),
                  _const_spec((1, GK_W)), _const_spec((1, DQ_W)), _const_spec((1, DQ_W)),
                  pos, pos, _const_spec((LANES, LANES))],
        out_specs=out_specs,
        out_shape=out_shape,
        compiler_params=pltpu.CompilerParams(dimension_semantics=("arbitrary",),
                                             vmem_limit_bytes=VMEM_LIMIT),
        name="proj",
    )(x, w["attn_norm_g"], w["w_proj"], w["w_lr"], w["w_gk2"], w["b_gk"], w["q_norm_g"],
      w["k_norm_g"], pos_cos, pos_sin, w["group_ind"])


GLA_GROUP = 128
GLA_LEVELS = GLA_CHUNK.bit_length() - 1


def _gla_constants():
    n = GLA_GROUP
    t = np.arange(n)[:, None]
    u = np.arange(n)[None, :]
    mats = [((t // GLA_CHUNK == u // GLA_CHUNK) & (u <= t)).astype(np.float32)]
    masks = []
    for lvl in range(GLA_LEVELS):
        h = 1 << lvl
        ref = (t // (2 * h)) * (2 * h) + h - 1
        upper = (t & h) != 0
        mats.append(np.where(upper & (u > ref) & (u <= t), 1.0, 0.0)
                    - np.where(~upper & (u > t) & (u <= ref), 1.0, 0.0))
        masks.append(((t // (2 * h) == u // (2 * h)) & upper & ((u & h) == 0)).astype(np.float32))
    head = np.arange(GK_W)[:, None] // GLA_DK == np.arange(GV_W)[None, :] // GLA_DV
    return (jnp.asarray(np.concatenate(mats, 0), BF16), jnp.asarray(np.concatenate(masks, 0), BF16),
            jnp.asarray(head.astype(np.float32), BF16))


def _gla_kernel(q_ref, k_ref, v_ref, la_ref, s0_ref, mexp_ref, mask_ref, expand_ref, o_ref,
                sout_ref, s_scr, ex_scr, a_scr, u_scr, d_scr, sb_scr, *, n_groups):
    j = pl.program_id(1)
    n = GLA_GROUP
    pair_w = 2 * GLA_DK
    n_pairs = GLA_HEADS // 2
    per_group = n // GLA_CHUNK

    @pl.when(j == 0)
    def _():
        s_scr[...] = s0_ref[...]

    def head_only(x, h):
        lane = lax.broadcasted_iota(jnp.int32, x.shape, 1)
        keep = lane < GLA_DK if h % 2 == 0 else lane >= GLA_DK
        return jnp.where(keep, x, 0.0)

    grp = lambda g: slice(g * n, (g + 1) * n)
    chunk_rows = lambda c: slice(c * GLA_CHUNK, (c + 1) * GLA_CHUNK)

    def chunk_b(c):
        lo = (c % per_group) * GLA_CHUNK
        return ex_scr[c // per_group, lo:lo + GLA_CHUNK, :]

    for g in range(n_groups):
        la = la_ref[grp(g), :]
        la_hi = la.astype(BF16)
        la_lo = (la - la_hi.astype(F32)).astype(BF16)
        ex_scr[g, :n, :] = _dot(mexp_ref[:n, :], la_hi) + _dot(mexp_ref[:n, :], la_lo)
        ex_scr[g, n:, :] = _dot(mexp_ref[n:, :], la_hi)

    for g in range(n_groups):
        q = q_ref[grp(g), :]
        k = k_ref[grp(g), :]
        a = [None] * GLA_HEADS
        for lvl in range(GLA_LEVELS):
            d = ex_scr[g, (lvl + 1) * n:(lvl + 2) * n, :]
            ql = q * jnp.exp(jnp.minimum(d, 0.0))
            kl = (k * jnp.exp(jnp.minimum(-d, 0.0))).astype(BF16)
            msk = mask_ref[lvl * n:(lvl + 1) * n, :] > 0
            for h in range(GLA_HEADS):
                tile = slice((h // 2) * pair_w, (h // 2 + 1) * pair_w)
                t = _dot_nt(head_only(ql[:, tile], h).astype(BF16), kl[:, tile])
                t = jnp.where(msk, t, 0.0)
                a[h] = t if a[h] is None else a[h] + t
        for h in range(GLA_HEADS):
            a_scr[g, h] = a[h].astype(BF16)

    for g in range(n_groups):
        v = v_ref[grp(g), :]
        vb = v.astype(BF16)
        o = _dot((q_ref[grp(g), :] * k_ref[grp(g), :]).astype(BF16), expand_ref[...]) * v
        o_ref[grp(g), :] = o + jnp.concatenate(
            [_dot(a_scr[g, h], vb[:, h * GLA_DV:(h + 1) * GLA_DV]) for h in range(GLA_HEADS)],
            axis=1)

    for c in range(n_groups * per_group):
        r = chunk_rows(c)
        bc = chunk_b(c)
        b_last = bc[GLA_CHUNK - 1:GLA_CHUNK, :]
        ke = (k_ref[r, :] * jnp.exp(b_last - bc)).astype(BF16)
        vb = v_ref[r, :].astype(BF16)
        decay = jnp.exp(jnp.broadcast_to(b_last, (GLA_DV, GK_W))).T
        for pair in range(n_pairs):
            tile = slice(pair * pair_w, (pair + 1) * pair_w)
            upd = []
            for h in (2 * pair, 2 * pair + 1):
                kv = _dot_tn(ke[:, tile], vb[:, h * GLA_DV:(h + 1) * GLA_DV])
                upd.append(kv[(h % 2) * GLA_DK:(h % 2 + 1) * GLA_DK])
            u_scr[c, pair] = jnp.concatenate(upd, axis=0)
            d_scr[c, pair] = decay[tile, :]

    for c in range(n_groups * per_group):
        for pair in range(n_pairs):
            tile = slice(pair * pair_w, (pair + 1) * pair_w)
            s_pair = s_scr[tile, :]
            sb_scr[c, pair] = s_pair.astype(BF16)
            s_scr[tile, :] = d_scr[c, pair] * s_pair + u_scr[c, pair]

    for c in range(n_groups * per_group):
        r = chunk_rows(c)
        qe = q_ref[r, :] * jnp.exp(chunk_b(c))
        heads = []
        for h in range(GLA_HEADS):
            tile = slice((h // 2) * pair_w, (h // 2 + 1) * pair_w)
            heads.append(_dot(head_only(qe[:, tile], h).astype(BF16), sb_scr[c, h // 2]))
        o_ref[r, :] += jnp.concatenate(heads, axis=1)

    @pl.when(j == pl.num_programs(1) - 1)
    def _():
        sout_ref[...] = s_scr[...]


def _gla(q, k, v, la, s0, *, block):
    b, t, _ = q.shape
    assert t % block == 0 and block % GLA_GROUP == 0
    mexp, masks, expand = _gla_constants()
    n_groups = block // GLA_GROUP
    n_chunks = block // GLA_CHUNK
    n_pairs = GLA_HEADS // 2
    grid = (b, t // block)
    tok = lambda width: pl.BlockSpec((None, block, width), lambda i, j: (i, j, 0))
    st = pl.BlockSpec((None, GK_W, GLA_DV), lambda i, j: (i, 0, 0))
    return pl.pallas_call(
        functools.partial(_gla_kernel, n_groups=n_groups),
        grid=grid,
        in_specs=[tok(GK_W), tok(GK_W), tok(GV_W), tok(GK_W), st, _const_spec(mexp.shape),
                  _const_spec(masks.shape), _const_spec(expand.shape)],
        out_specs=[tok(GV_W), st],
        out_shape=[jax.ShapeDtypeStruct((b, t, GV_W), F32),
                   jax.ShapeDtypeStruct((b, GK_W, GLA_DV), F32)],
        scratch_shapes=[pltpu.VMEM((GK_W, GLA_DV), F32),
                        pltpu.VMEM((n_groups, (GLA_LEVELS + 1) * GLA_GROUP, GK_W), F32),
                        pltpu.VMEM((n_groups, GLA_HEADS, GLA_GROUP, GLA_GROUP), BF16),
                        pltpu.VMEM((n_chunks, n_pairs, 2 * GLA_DK, GLA_DV), F32),
                        pltpu.VMEM((n_chunks, n_pairs, 2 * GLA_DK, GLA_DV), F32),
                        pltpu.VMEM((n_chunks, n_pairs, 2 * GLA_DK, GLA_DV), BF16)],
        compiler_params=pltpu.CompilerParams(dimension_semantics=("arbitrary", "arbitrary"),
                                             vmem_limit_bytes=VMEM_LIMIT),
        name="gla",
    )(q, k, v, la, s0, mexp, masks, expand)


def _lambda(lp_ref, lam_init):
    lp = lp_ref[...]
    s1 = jnp.sum(lp[0:1, :] * lp[1:2, :], axis=1, keepdims=True)
    s2 = jnp.sum(lp[2:3, :] * lp[3:4, :], axis=1, keepdims=True)
    return jnp.exp(s1) - jnp.exp(s2) + lam_init


_RELAYOUT_ROWS = 512
ATTN_CHAIN = 256
ATTN_SLOTS = 32
ONES_ROWS = 16


def _attn_prompt_kernel(lp_ref, q_ref, kb_ref, vt_ref, o_ref, vt_scr, qst_scr, st_scr, m_scr,
                        acc_scr, *, tq, seq, lam_init):
    i = pl.program_id(2)
    cw = ATTN_CHAIN
    per_sub = tq // cw
    n_chains = 2 * per_sub

    @pl.when(i == 0)
    def _():
        vt_scr[:DIFF_DV, :] = vt_ref[...]
        vt_scr[DIFF_DV:, :] = jnp.ones((ONES_ROWS, seq), BF16)

    feat = lax.broadcasted_iota(jnp.int32, (2 * DIFF_DH, _RELAYOUT_ROWS), 0)
    first = feat < DIFF_DH
    for c in range(tq // _RELAYOUT_ROWS):
        r = slice(c * _RELAYOUT_ROWS, (c + 1) * _RELAYOUT_ROWS)
        qt = (q_ref[r, :] * (DIFF_DH ** -0.5 * math.log2(math.e))).T
        qst_scr[:, r] = jnp.where(first, qt, 0.0).astype(BF16)
        qst_scr[:, tq + c * _RELAYOUT_ROWS:tq + (c + 1) * _RELAYOUT_ROWS] = (
            jnp.where(first, 0.0, qt).astype(BF16))

    m_scr[...] = jnp.full(m_scr.shape, -jnp.inf, F32)
    acc_scr[...] = jnp.zeros(acc_scr.shape, F32)

    kpos = lax.broadcasted_iota(jnp.int32, (cw, cw), 0)
    qpos = lax.broadcasted_iota(jnp.int32, (cw, cw), 1)
    causal = kpos <= qpos

    def scores(slot, r, c):
        st_scr[:, slot * cw:(slot + 1) * cw] = _dot(kb_ref[r, :], qst_scr[:, c * cw:(c + 1) * cw])

    def softmax_pv(slot, r, c, masked):
        cols = slice(c * cw, (c + 1) * cw)
        st = st_scr[:, slot * cw:(slot + 1) * cw]
        if masked:
            st = jnp.where(causal, st, -jnp.inf)
        m_old = m_scr[:, cols]
        m_new = jnp.maximum(m_old, jnp.max(st, axis=0, keepdims=True))
        p = jnp.exp2(st - m_new).astype(BF16)
        acc_scr[:, cols] = jnp.exp2(m_old - m_new) * acc_scr[:, cols] + _dot(vt_scr[:, r], p)
        m_scr[:, cols] = m_new

    def run(items):
        assert len(items) <= ATTN_SLOTS
        for slot, (r, c, _) in enumerate(items):
            scores(slot, r, c)
        for slot, (r, c, masked) in enumerate(items):
            softmax_pv(slot, r, c, masked)

    blocks_per_trip = ATTN_SLOTS // n_chains

    def body(j, carry):
        items = []
        for u in range(blocks_per_trip):
            r = pl.ds(pl.multiple_of((j * blocks_per_trip + u) * cw, cw), cw)
            items += [(r, c, False) for c in range(n_chains)]
        run(items)
        return carry

    lax.fori_loop(0, i * per_sub // blocks_per_trip, body, 0)

    items = []
    for jj in range(per_sub):
        r = pl.ds(pl.multiple_of(i * tq + jj * cw, cw), cw)
        block = [(r, c, c % per_sub == jj) for c in range(n_chains) if c % per_sub >= jj]
        if len(items) + len(block) > ATTN_SLOTS:
            run(items)
            items = []
        items += block
    run(items)

    lam = _lambda(lp_ref, lam_init)
    for c in range(per_sub):
        c1 = slice(c * cw, (c + 1) * cw)
        c2 = slice(tq + c * cw, tq + (c + 1) * cw)
        o1 = acc_scr[:DIFF_DV, c1] / acc_scr[DIFF_DV:DIFF_DV + 1, c1]
        o2 = acc_scr[:DIFF_DV, c2] / acc_scr[DIFF_DV:DIFF_DV + 1, c2]
        o_ref[c1, :] = (o1 - lam * o2).T


def _attn_prompt(q, kb, vt, lam_params, lam_init, *, tq):
    b, s, _ = q.shape
    per_sub = tq // ATTN_CHAIN
    assert tq % ATTN_CHAIN == 0 and ATTN_SLOTS % (2 * per_sub) == 0
    assert per_sub % (ATTN_SLOTS // (2 * per_sub)) == 0
    grid = (b, DIFF_HEADS, s // tq)
    width = 2 * DIFF_DH
    qspec = pl.BlockSpec((None, tq, width), lambda bi, h, i: (bi, i, h))
    kspec = pl.BlockSpec((None, s, width), lambda bi, h, i: (bi, 0, h))
    vtspec = pl.BlockSpec((None, DIFF_DV, s), lambda bi, h, i: (bi, h, 0))
    return pl.pallas_call(
        functools.partial(_attn_prompt_kernel, tq=tq, seq=s, lam_init=lam_init),
        grid=grid,
        in_specs=[_const_spec((4, DIFF_DH)), qspec, kspec, vtspec],
        out_specs=qspec,
        out_shape=jax.ShapeDtypeStruct((b, s, DV_W), F32),
        scratch_shapes=[pltpu.VMEM((DIFF_DV + ONES_ROWS, s), BF16),
                        pltpu.VMEM((width, 2 * tq), BF16),
                        pltpu.VMEM((ATTN_CHAIN, ATTN_SLOTS * ATTN_CHAIN), F32),
                        pltpu.VMEM((1, 2 * tq), F32),
                        pltpu.VMEM((DIFF_DV + ONES_ROWS, 2 * tq), F32)],
        compiler_params=pltpu.CompilerParams(
            dimension_semantics=("arbitrary", "arbitrary", "arbitrary"),
            vmem_limit_bytes=VMEM_LIMIT),
        name="attn_prompt",
    )(lam_params, q, kb, vt)


def _head_norm(o, g):
    return jnp.concatenate(
        [_rms(o[:, h * LANES:(h + 1) * LANES], g) for h in range(o.shape[1] // LANES)], axis=1)


def _ffn_mix(x_ref, go_ref, gg_ref, do_ref, gn_ref, dn_ref, wout_ref, lam_init):
    gg = gg_ref[...]
    go = _head_norm(go_ref[...], gn_ref[...]) * (gg * jax.nn.sigmoid(gg))
    do = _head_norm(do_ref[...], dn_ref[...]) * (1.0 - lam_init)
    mix = jnp.concatenate([go, do], axis=1).astype(BF16)
    return x_ref[...] + _dot(mix, wout_ref[...])


def _ffn_gate_up(h, fn_ref, wg_ref, wu_ref):
    hn = _rms(h, fn_ref[...]).astype(BF16)
    return _dot(hn, wg_ref[...]), _dot(hn, wu_ref[...])


def _ffn_finish(h, gate, up, p_ref, wd_ref, pn_ref, wpg_ref, wpp_ref):
    ple = _dot(p_ref[...].astype(BF16), wpp_ref[...])
    h = h + _dot(((gate * jax.nn.sigmoid(gate)) * up).astype(BF16), wd_ref[...])
    pg = jax.nn.sigmoid(_dot(_rms(h, pn_ref[...]).astype(BF16), wpg_ref[...]))
    return h + pg * ple


N_FFN_INPUTS = 15


def _attn_sample_kernel(pt_ref, lp_ref, q_ref, kn_ref, vn_ref, ckt_hbm, cv_hbm, *refs, pages, t_new,
                        lam_init, with_ffn):
    if with_ffn:
        (x_ref, go_ref, gg_ref, do_ref, p_ref, gn_ref, dn_ref, wout_ref, fn_ref, wg_ref, wu_ref,
         wd_ref, pn_ref, wpg_ref, wpp_ref) = refs[:N_FFN_INPUTS]
        o_ref, y_ref, qs_scr, m_scr, l_scr, acc_scr, kbuf, vbuf, sem = refs[N_FFN_INPUTS:]
    else:
        o_ref, qs_scr, m_scr, l_scr, acc_scr, kbuf, vbuf, sem = refs
    p = pl.program_id(1)
    per_row = pl.num_programs(1)
    step = pl.program_id(0) * per_row + p
    last_step = pl.num_programs(0) * per_row - 1
    slot = step % 2
    n_rows = 2 * DIFF_HEADS * t_new
    head_rows = 2 * t_new

    def page_copies(of_step, into_slot):
        row = of_step // per_row
        col = (of_step % per_row) * pages
        copies = []
        for g in range(pages):
            page = pt_ref[row, col + g]
            copies.append(pltpu.make_async_copy(ckt_hbm.at[page], kbuf.at[into_slot, g],
                                                sem.at[into_slot, 0, g]))
            copies.append(pltpu.make_async_copy(cv_hbm.at[page], vbuf.at[into_slot, g],
                                                sem.at[into_slot, 1, g]))
        return copies

    @pl.when(step == 0)
    def _():
        for c in page_copies(0, 0):
            c.start()

    for c in page_copies(step, slot):
        c.wait()

    @pl.when(p == 0)
    def _():
        q = q_ref[...] * (DIFF_DH ** -0.5)
        lane = lax.broadcasted_iota(jnp.int32, (t_new, DQ_W), 1) // DIFF_DH
        qs = jnp.concatenate([jnp.where(lane == g, q, 0.0) for g in range(2 * DIFF_HEADS)],
                             axis=0).astype(BF16)
        qs_scr[...] = qs
        s = _dot_nt(qs, kn_ref[...].astype(BF16))
        qpos = lax.broadcasted_iota(jnp.int32, (n_rows, t_new), 0) % t_new
        kpos = lax.broadcasted_iota(jnp.int32, (n_rows, t_new), 1)
        s = jnp.where(kpos <= qpos, s, -jnp.inf)
        m = jnp.max(s, axis=1, keepdims=True)
        e = jnp.exp(s - m)
        m_scr[...] = m
        l_scr[...] = jnp.sum(e, axis=1, keepdims=True)
        eb = e.astype(BF16)
        vn = vn_ref[...].astype(BF16)
        acc_scr[...] = jnp.concatenate(
            [_dot(eb[h * head_rows:(h + 1) * head_rows, :], vn[:, h * DIFF_DV:(h + 1) * DIFF_DV])
             for h in range(DIFF_HEADS)], axis=0)

    if with_ffn:
        h_mid = _ffn_mix(x_ref, go_ref, gg_ref, do_ref, gn_ref, dn_ref, wout_ref, lam_init)
    qs = qs_scr[...]
    s = jnp.concatenate(
        [_dot(qs, jnp.concatenate([kbuf[slot, g], kbuf[slot, g + 1]], axis=1).astype(BF16))
         for g in range(0, pages, 2)], axis=1)
    for c in page_copies(jnp.minimum(step + 1, last_step), 1 - slot):
        c.start()
    if with_ffn:
        gate, up = _ffn_gate_up(h_mid, fn_ref, wg_ref, wu_ref)
    m_old = m_scr[...]
    m_new = jnp.maximum(m_old, jnp.max(s, axis=1, keepdims=True))
    alpha = jnp.exp(m_old - m_new)
    e = jnp.exp(s - m_new)
    l_scr[...] = alpha * l_scr[...] + jnp.sum(e, axis=1, keepdims=True)
    eb = e.astype(BF16)
    pv = []
    for h in range(DIFF_HEADS):
        rows = slice(h * head_rows, (h + 1) * head_rows)
        acc_h = None
        for g in range(0, pages, 2):
            v_h = jnp.concatenate(
                [vbuf[slot, g, pl.ds(h, PAGE_SIZE, stride=DIFF_HEADS), :],
                 vbuf[slot, g + 1, pl.ds(h, PAGE_SIZE, stride=DIFF_HEADS), :]],
                axis=0).astype(BF16)
            t = _dot(eb[rows, g * PAGE_SIZE:(g + 2) * PAGE_SIZE], v_h)
            acc_h = t if acc_h is None else acc_h + t
        pv.append(acc_h)
    if with_ffn:
        y_ref[...] = _ffn_finish(h_mid, gate, up, p_ref, wd_ref, pn_ref, wpg_ref, wpp_ref)
    acc_scr[...] = alpha * acc_scr[...] + jnp.concatenate(pv, axis=0)
    m_scr[...] = m_new

    @pl.when(p == pl.num_programs(1) - 1)
    def _():
        out = acc_scr[...] / l_scr[...]
        lam = _lambda(lp_ref, lam_init)
        heads = []
        for h in range(DIFF_HEADS):
            r1 = h * head_rows
            r2 = r1 + t_new
            heads.append(out[r1:r1 + t_new, :] - lam * out[r2:r2 + t_new, :])
        o_ref[...] = jnp.concatenate(heads, axis=1)

    @pl.when(step == last_step)
    def _():
        for c in page_copies(last_step, 1 - slot):
            c.wait()


def _ffn_operands(x, go, gg, do, p, w):
    return (x, go, gg, do, p, w["gla_norm_g"], w["diff_norm_g"], w["w_out"], w["ffn_norm_g"],
            w["w_ffn_gate"], w["w_ffn_up"], w["w_ffn_down"], w["ple_norm_g"], w["w_ple_gate"],
            w["w_ple_proj"])


def _ffn_in_specs(tok, ple, d_ff):
    return [tok(D_MODEL), tok(GV_W), tok(GV_W), tok(DV_W), tok(ple),
            _const_spec((1, GLA_DV)), _const_spec((1, DIFF_DV)),
            _const_spec((D_MODEL, D_MODEL)), _const_spec((1, D_MODEL)),
            _const_spec((D_MODEL, d_ff)), _const_spec((D_MODEL, d_ff)),
            _const_spec((d_ff, D_MODEL)), _const_spec((1, D_MODEL)),
            _const_spec((D_MODEL, D_MODEL)), _const_spec((ple, D_MODEL))]


def _attn_sample(q, k_new, v_new, cache_kt, cache_v, page_table, lam_params, lam_init, *, pages,
                 ffn=None):
    db, t_new, _ = q.shape
    n_pages = page_table.shape[1]
    per_row = n_pages // pages
    grid = (db, per_row)
    n_rows = 2 * DIFF_HEADS * t_new
    tok = pl.BlockSpec((None, t_new, DQ_W), lambda b, p, pt: (b, 0, 0))

    hbm = pl.BlockSpec(memory_space=pl.ANY)
    in_specs = [pl.BlockSpec((4, DIFF_DH), lambda b, p, pt: (0, 0)), tok, tok, tok, hbm, hbm]
    operands = [page_table, lam_params, q, k_new, v_new, cache_kt, cache_v]
    out_specs = [tok]
    out_shape = [jax.ShapeDtypeStruct((db, t_new, DV_W), F32)]
    if ffn is not None:
        x, w = ffn[0], ffn[5]
        n = x.shape[0]
        tm = n // (db * per_row)
        assert tm * db * per_row == n and tm % 8 == 0
        ftok = lambda width: pl.BlockSpec((tm, width), lambda b, p, pt: (b * per_row + p, 0))
        in_specs += _ffn_in_specs(ftok, ffn[4].shape[1], w["w_ffn_gate"].shape[1])
        operands += list(_ffn_operands(*ffn))
        out_specs.append(ftok(D_MODEL))
        out_shape.append(jax.ShapeDtypeStruct((n, D_MODEL), F32))
    grid_spec = pltpu.PrefetchScalarGridSpec(
        num_scalar_prefetch=1,
        grid=grid,
        in_specs=in_specs,
        out_specs=out_specs,
        scratch_shapes=[pltpu.VMEM((n_rows, DQ_W), BF16), pltpu.VMEM((n_rows, 1), F32),
                        pltpu.VMEM((n_rows, 1), F32), pltpu.VMEM((n_rows, DIFF_DV), F32),
                        pltpu.VMEM((2, pages, DQ_W, PAGE_SIZE), F32),
                        pltpu.VMEM((2, pages, PAGE_SIZE * DIFF_HEADS, DIFF_DV), F32),
                        pltpu.SemaphoreType.DMA((2, 2, pages))],
    )
    res = pl.pallas_call(
        functools.partial(_attn_sample_kernel, pages=pages, t_new=t_new, lam_init=lam_init,
                          with_ffn=ffn is not None),
        grid_spec=grid_spec,
        out_shape=out_shape,
        compiler_params=pltpu.CompilerParams(dimension_semantics=("arbitrary", "arbitrary"),
                                             vmem_limit_bytes=VMEM_LIMIT),
        name="attn_sample",
    )(*operands)
    return res if ffn is not None else res[0]


def _out_ffn_kernel(x_ref, go_ref, gg_ref, do_ref, p_ref, gn_ref, dn_ref, wout_ref, fn_ref,
                    wg_ref, wu_ref, wd_ref, pn_ref, wpg_ref, wpp_ref, y_ref, *, lam_init):
    h = _ffn_mix(x_ref, go_ref, gg_ref, do_ref, gn_ref, dn_ref, wout_ref, lam_init)
    gate, up = _ffn_gate_up(h, fn_ref, wg_ref, wu_ref)
    y_ref[...] = _ffn_finish(h, gate, up, p_ref, wd_ref, pn_ref, wpg_ref, wpp_ref)


def _out_ffn(x, go, gg, do, p, w, lam_init, *, tm):
    n = x.shape[0]
    tok = lambda width: pl.BlockSpec((tm, width), lambda i: (i, 0))
    return pl.pallas_call(
        functools.partial(_out_ffn_kernel, lam_init=lam_init),
        grid=(n // tm,),
        in_specs=_ffn_in_specs(tok, p.shape[1], w["w_ffn_gate"].shape[1]),
        out_specs=tok(D_MODEL),
        out_shape=jax.ShapeDtypeStruct((n, D_MODEL), F32),
        compiler_params=pltpu.CompilerParams(dimension_semantics=("arbitrary",),
                                             vmem_limit_bytes=VMEM_LIMIT),
        name="out_ffn",
    )(*_ffn_operands(x, go, gg, do, p, w))


def _rope_tables(pos):
    half = DIFF_DH // 2
    inv = ROPE_THETA ** (-jnp.arange(half, dtype=F32) / half)
    ang = pos.astype(F32)[:, None] * inv[None, :]
    cos, sin = jnp.cos(ang), jnp.sin(ang)
    cos = jnp.concatenate([cos, cos], axis=1)
    sin = jnp.concatenate([-sin, sin], axis=1)
    reps = LANES // DIFF_DH
    return jnp.tile(cos, (1, reps)), jnp.tile(sin, (1, reps))


def _layer_weights(i, attn_norm_g, w_in, w_gk2, b_gk, q_norm_g, k_norm_g, lam_params, gla_norm_g,
                   diff_norm_g, w_out, ffn_norm_g, w_ffn_gate, w_ffn_up, w_ffn_down, ple_norm_g,
                   w_ple_gate, w_ple_proj):
    lr0 = 2 * GK_W + 2 * GV_W
    wi = w_in[i]
    w_lr = jnp.pad(wi[:, lr0:lr0 + GLA_GATE_RANK], ((0, 0), (0, LR_PAD - GLA_GATE_RANK)))
    gid = jnp.arange(LANES) // DIFF_DH
    dq0 = lr0 + GLA_GATE_RANK
    row = lambda a: a.reshape(1, -1).astype(F32)
    return {
        "attn_norm_g": row(attn_norm_g[i]),
        "w_proj": jnp.concatenate([wi[:, dq0:dq0 + 2 * DQ_W], wi[:, :lr0], wi[:, dq0 + 2 * DQ_W:]],
                                  axis=1).astype(BF16),
        "w_lr": w_lr.astype(BF16),
        "w_gk2": jnp.pad(w_gk2[i], ((0, LR_PAD - GLA_GATE_RANK), (0, 0))).astype(BF16),
        "b_gk": row(b_gk[i]),
        "q_norm_g": row(jnp.tile(q_norm_g[i], DQ_W // DIFF_DH)),
        "k_norm_g": row(jnp.tile(k_norm_g[i], DQ_W // DIFF_DH)),
        "group_ind": (gid[:, None] == gid[None, :]).astype(BF16),
        "lam_params": lam_params[i].astype(F32),
        "gla_norm_g": row(gla_norm_g[i]),
        "diff_norm_g": row(diff_norm_g[i]),
        "w_out": w_out[i].astype(BF16),
        "ffn_norm_g": row(ffn_norm_g[i]),
        "w_ffn_gate": w_ffn_gate[i].astype(BF16),
        "w_ffn_up": w_ffn_up[i].astype(BF16),
        "w_ffn_down": w_ffn_down[i].astype(BF16),
        "ple_norm_g": row(ple_norm_g[i]),
        "w_ple_gate": w_ple_gate[i].astype(BF16),
        "w_ple_proj": w_ple_proj[i].astype(BF16),
    }


def kernel(x_prompt, x_sample, p_prompt, p_sample, cache_k, cache_v, state_gla, page_table, attn_norm_g, w_in, w_gk2, b_gk, q_norm_g, k_norm_g, lam_params, gla_norm_g, diff_norm_g, w_out, ffn_norm_g, w_ffn_gate, w_ffn_up, w_ffn_down, ple_norm_g, w_ple_gate, w_ple_proj):
    B, S, _ = x_prompt.shape
    DB, T, _ = x_sample.shape
    depth = w_in.shape[0]
    n_pool = cache_k.shape[1]
    n_pages = page_table.shape[1]
    past_len = n_pages * PAGE_SIZE

    tm_p = 512
    cos_p, sin_p = _rope_tables(jnp.arange(S))
    cos_s, sin_s = _rope_tables(jnp.tile(past_len + jnp.arange(T), DB))
    t_pad = GLA_GROUP
    cache_k2 = jnp.transpose(cache_k, (0, 1, 3, 4, 5, 2)).reshape(depth * n_pool, DQ_W, PAGE_SIZE)
    cache_v2 = cache_v.reshape(depth * n_pool, PAGE_SIZE * DIFF_HEADS, DIFF_DV)

    hp = x_prompt.reshape(B * S, D_MODEL)
    hs = x_sample.reshape(DB * T, D_MODEL)
    kp_l, vp_l, gp_l, ks_l, vs_l, gs_l = [], [], [], [], [], []
    for i in range(depth):
        lam_init = 0.8 - 0.6 * math.exp(-0.3 * i)
        w = _layer_weights(i, attn_norm_g, w_in, w_gk2, b_gk, q_norm_g, k_norm_g, lam_params,
                           gla_norm_g, diff_norm_g, w_out, ffn_norm_g, w_ffn_gate, w_ffn_up,
                           w_ffn_down, ple_norm_g, w_ple_gate, w_ple_proj)

        gq, gk, gv, gg, la, dq, dkb, dkt, dv4, dvt = _proj(hp, cos_p, sin_p, w, tm=tm_p,
                                                           pos_period_blocks=S // tm_p, seq=S)
        r3 = lambda a: a.reshape(B, S, a.shape[-1])
        s0 = jnp.zeros((B, GK_W, GLA_DV), F32)
        gla_o, gla_s = _gla(r3(gq), r3(gk), r3(gv), r3(la), s0, block=512)
        diff_o = _attn_prompt(r3(dq), r3(dkb), dvt, w["lam_params"], lam_init, tq=1024)
        ffn_prompt = (hp, gla_o.reshape(B * S, GV_W), gg, diff_o.reshape(B * S, DV_W),
                      p_prompt[i].reshape(B * S, -1), w)
        kp_l.append(jnp.transpose(dkt.reshape(B, DIFF_HEADS, 2, DIFF_DH, S), (0, 4, 1, 2, 3)))
        vp_l.append(dv4.reshape(B, S, DIFF_HEADS, DIFF_DV))
        gp_l.append(gla_s.reshape(B, GLA_HEADS, GLA_DK, GLA_DV))

        gq, gk, gv, gg, la, dq, dk, dv = _proj(hs, cos_s, sin_s, w, tm=DB * T, pos_period_blocks=1)
        r3 = lambda a: a.reshape(DB, T, a.shape[-1])
        padt = lambda a: jnp.pad(r3(a), ((0, 0), (0, t_pad - T), (0, 0)))
        s0 = state_gla[i].reshape(DB, GK_W, GLA_DV)
        gla_o, gla_s = _gla(padt(gq), padt(gk), padt(gv), padt(la), s0, block=t_pad)
        gla_o = gla_o[:, :T]
        diff_o, hp = _attn_sample(r3(dq), r3(dk), r3(dv), cache_k2, cache_v2,
                                  page_table + i * n_pool, w["lam_params"], lam_init, pages=16,
                                  ffn=ffn_prompt)
        hs = _out_ffn(hs, gla_o.reshape(DB * T, GV_W), gg, diff_o.reshape(DB * T, DV_W),
                      p_sample[i].reshape(DB * T, -1), w, lam_init, tm=DB * T)
        ks_l.append(dk.reshape(DB, T, DIFF_HEADS, 2, DIFF_DH))
        vs_l.append(dv.reshape(DB, T, DIFF_HEADS, DIFF_DV))
        gs_l.append(gla_s.reshape(DB, GLA_HEADS, GLA_DK, GLA_DV))

    return (hp.reshape(B, S, D_MODEL), hs.reshape(DB, T, D_MODEL),
            jnp.stack(kp_l), jnp.stack(vp_l), jnp.stack(gp_l),
            jnp.stack(ks_l), jnp.stack(vs_l), jnp.stack(gs_l))
```

```python
import functools
import math

import jax
import jax.numpy as jnp
import numpy as np
from jax import lax
from jax.experimental import pallas as pl
from jax.experimental.pallas import tpu as pltpu

F32 = jnp.float32
BF16 = jnp.bfloat16

D_MODEL = 1024
GLA_HEADS = 4
GLA_DV = 128
GLA_DK = 64
GLA_GATE_RANK = 16
GLA_GATE_NORMALIZER = 16.0
GLA_CHUNK = 64
DIFF_HEADS = 4
DIFF_DV = 128
DIFF_DH = 64
ROPE_THETA = 10000.0
PAGE_SIZE = 128
EPS = 1e-6
GK_W = GLA_HEADS * GLA_DK
GV_W = GLA_HEADS * GLA_DV
DQ_W = DIFF_HEADS * 2 * DIFF_DH
DV_W = DIFF_HEADS * DIFF_DV
LANES = 128
LR_PAD = LANES
PROJ_W = 2 * GK_W + 2 * GV_W + 2 * DQ_W + DV_W
VMEM_LIMIT = 56 * 1024 * 1024

PROJ_TM = 512
GLA_BLOCK = 512
ATTN_TQ = 1024
PAGES_PER_STEP = 16


def _dot(a, b):
    return jnp.dot(a, b, preferred_element_type=F32)


def _dot_nt(a, b):
    return lax.dot_general(a, b, (((1,), (1,)), ((), ())), preferred_element_type=F32)


def _dot_tn(a, b):
    return lax.dot_general(a, b, (((0,), (0,)), ((), ())), preferred_element_type=F32)


def _rms(x, g):
    return x * lax.rsqrt(jnp.mean(x * x, axis=-1, keepdims=True) + EPS) * g


def _const_spec(shape):
    nd = len(shape)
    return pl.BlockSpec(shape, lambda *_: (0,) * nd, pipeline_mode=pl.Buffered(1))


def _proj_kernel(x_ref, ng_ref, w_ref, wlr_ref, wgk2_ref, bgk_ref, qg_ref, kg_ref, cos_ref,
                 sin_ref, ind_ref, gq_ref, gk_ref, gv_ref, gg_ref, la_ref, dq_ref, *kv_refs,
                 feature_major):
    x = x_ref[...]
    xn = _rms(x, ng_ref[...]).astype(BF16)
    qk = _dot(xn, w_ref[:, :2 * DQ_W])
    dq = qk[:, :DQ_W]
    dk = qk[:, DQ_W:]

    tm = x.shape[0]
    cos = jnp.concatenate([cos_ref[...]] * (DQ_W // LANES), axis=1)
    sin = jnp.concatenate([sin_ref[...]] * (DQ_W // LANES), axis=1)
    lane = lax.broadcasted_iota(jnp.int32, (tm, DQ_W), 1)
    upper = (lane & (DIFF_DH // 2)) != 0
    ind = ind_ref[...]

    def norm_rope(y, g):
        ms = jnp.concatenate(
            [_dot((y[:, t * LANES:(t + 1) * LANES] ** 2).astype(BF16), ind)
             for t in range(DQ_W // LANES)], axis=1) * (1.0 / DIFF_DH)
        yn = y * lax.rsqrt(ms + EPS) * g
        partner = jnp.where(upper, pltpu.roll(yn, DIFF_DH // 2, 1),
                            pltpu.roll(yn, DQ_W - DIFF_DH // 2, 1))
        return yn * cos + partner * sin

    dq_ref[...] = norm_rope(dq, qg_ref[...])
    dk = norm_rope(dk, kg_ref[...])
    if feature_major:
        dkb_ref, dkt_ref, dv4_ref, dvt_ref = kv_refs
        dkb_ref[...] = dk.astype(BF16)
        dkt_ref[...] = dk.T
    else:
        dk_ref, dv_ref = kv_refs
        dk_ref[...] = dk

    glr = _dot(xn, wlr_ref[...])
    z = _dot(glr.astype(BF16), wgk2_ref[...]) + bgk_ref[...]
    logsig = jnp.minimum(z, 0.0) - jnp.log(1.0 + jnp.exp(-jnp.abs(z)))
    la_ref[...] = logsig * (1.0 / GLA_GATE_NORMALIZER)

    rest = _dot(xn, w_ref[:, 2 * DQ_W:])
    o = 0
    gq_ref[...] = rest[:, o:o + GK_W] * (GLA_DK ** -0.5); o += GK_W
    gk_ref[...] = rest[:, o:o + GK_W]; o += GK_W
    gv_ref[...] = rest[:, o:o + GV_W]; o += GV_W
    gg_ref[...] = rest[:, o:o + GV_W]; o += GV_W
    dv = rest[:, o:o + DV_W]
    if feature_major:
        for h in range(DIFF_HEADS):
            dv4_ref[pl.ds(h, tm, stride=DIFF_HEADS), :] = dv[:, h * DIFF_DV:(h + 1) * DIFF_DV]
        dvt_ref[...] = dv.T.astype(BF16)
    else:
        dv_ref[...] = dv


def _proj(x, pos_cos, pos_sin, w, *, tm, pos_period_blocks, seq=None):
    n = x.shape[0]
    grid = (n // tm,)
    tok = lambda width: pl.BlockSpec((tm, width), lambda i: (i, 0))
    pos = pl.BlockSpec((tm, LANES), lambda i: (i % pos_period_blocks, 0))
    out_widths = (GK_W, GK_W, GV_W, GV_W, GK_W, DQ_W)
    out_specs = [tok(wd) for wd in out_widths]
    out_shape = [jax.ShapeDtypeStruct((n, wd), F32) for wd in out_widths]
    if seq is None:
        out_specs += [tok(DQ_W), tok(DV_W)]
        out_shape += [jax.ShapeDtypeStruct((n, DQ_W), F32), jax.ShapeDtypeStruct((n, DV_W), F32)]
    else:
        per_seq = seq // tm
        fm = lambda width: pl.BlockSpec((None, width, tm), lambda i: (i // per_seq, 0, i % per_seq))
        out_specs += [tok(DQ_W), fm(DQ_W),
                      pl.BlockSpec((tm * DIFF_HEADS, DIFF_DV), lambda i: (i, 0)), fm(DV_W)]
        out_shape += [jax.ShapeDtypeStruct((n, DQ_W), BF16),
                      jax.ShapeDtypeStruct((n // seq, DQ_W, seq), F32),
                      jax.ShapeDtypeStruct((n * DIFF_HEADS, DIFF_DV), F32),
                      jax.ShapeDtypeStruct((n // seq, DV_W, seq), BF16)]
    return pl.pallas_call(
        functools.partial(_proj_kernel, feature_major=seq is not None),
        grid=grid,
        in_specs=[tok(D_MODEL), _const_spec((1, D_MODEL)), _const_spec((D_MODEL, PROJ_W)),
                  _const_spec((D_MODEL, LR_PAD)), _const_spec((LR_PAD, GK_W)),
                  _const_spec((1, GK_W)), _const_spec((1, DQ_W)), _const_spec((1, DQ_W)),
                  pos, pos, _const_spec((LANES, LANES))],
        out_specs=out_specs,
        out_shape=out_shape,
        compiler_params=pltpu.CompilerParams(dimension_semantics=("arbitrary",),
                                             vmem_limit_bytes=VMEM_LIMIT),
        name="proj",
    )(x, w["attn_norm_g"], w["w_proj"], w["w_lr"], w["w_gk2"], w["b_gk"], w["q_norm_g"],
      w["k_norm_g"], pos_cos, pos_sin, w["group_ind"])


GLA_GROUP = 128
GLA_LEVELS = GLA_CHUNK.bit_length() - 1


def _gla_constants():
    n = GLA_GROUP
    t = np.arange(n)[:, None]
    u = np.arange(n)[None, :]
    mats = [((t // GLA_CHUNK == u // GLA_CHUNK) & (u <= t)).astype(np.float32)]
    masks = []
    for lvl in range(GLA_LEVELS):
        h = 1 << lvl
        ref = (t // (2 * h)) * (2 * h) + h - 1
        upper = (t & h) != 0
        mats.append(np.where(upper & (u > ref) & (u <= t), 1.0, 0.0)
                    - np.where(~upper & (u > t) & (u <= ref), 1.0, 0.0))
        masks.append(((t // (2 * h) == u // (2 * h)) & upper & ((u & h) == 0)).astype(np.float32))
    head = np.arange(GK_W)[:, None] // GLA_DK == np.arange(GV_W)[None, :] // GLA_DV
    return (jnp.asarray(np.concatenate(mats, 0), BF16), jnp.asarray(np.concatenate(masks, 0), BF16),
            jnp.asarray(head.astype(np.float32), BF16))


def _gla_kernel(q_ref, k_ref, v_ref, la_ref, s0_ref, mexp_ref, mask_ref, expand_ref, o_ref,
                sout_ref, s_scr, ex_scr, a_scr, u_scr, d_scr, sb_scr, *, n_groups):
    j = pl.program_id(1)
    n = GLA_GROUP
    pair_w = 2 * GLA_DK
    n_pairs = GLA_HEADS // 2
    per_group = n // GLA_CHUNK

    @pl.when(j == 0)
    def _():
        s_scr[...] = s0_ref[...]

    def head_only(x, h):
        lane = lax.broadcasted_iota(jnp.int32, x.shape, 1)
        keep = lane < GLA_DK if h % 2 == 0 else lane >= GLA_DK
        return jnp.where(keep, x, 0.0)

    grp = lambda g: slice(g * n, (g + 1) * n)
    chunk_rows = lambda c: slice(c * GLA_CHUNK, (c + 1) * GLA_CHUNK)

    def chunk_b(c):
        lo = (c % per_group) * GLA_CHUNK
        return ex_scr[c // per_group, lo:lo + GLA_CHUNK, :]

    for g in range(n_groups):
        la = la_ref[grp(g), :] * math.log2(math.e)
        la_hi = la.astype(BF16)
        la_lo = (la - la_hi.astype(F32)).astype(BF16)
        ex_scr[g, :n, :] = _dot(mexp_ref[:n, :], la_hi) + _dot(mexp_ref[:n, :], la_lo)
        ex_scr[g, n:, :] = _dot(mexp_ref[n:, :], la_hi)

    for g in range(n_groups):
        q = q_ref[grp(g), :]
        k = k_ref[grp(g), :]
        tiles = [slice((h // 2) * pair_w, (h // 2 + 1) * pair_w) for h in range(GLA_HEADS)]
        a = [0.0] * GLA_HEADS
        for lvl in range(GLA_LEVELS):
            d = ex_scr[g, (lvl + 1) * n:(lvl + 2) * n, :]
            ql = q * jnp.exp2(jnp.minimum(d, 0.0))
            kl = (k * jnp.exp2(jnp.minimum(-d, 0.0))).astype(BF16)
            msk = mask_ref[lvl * n:(lvl + 1) * n, :] > 0
            for h in range(GLA_HEADS):
                t = _dot_nt(head_only(ql[:, tiles[h]], h).astype(BF16), kl[:, tiles[h]])
                a[h] = jnp.where(msk, t, a[h])
        for h in range(GLA_HEADS):
            a_scr[g, h] = a[h].astype(BF16)

    for g in range(n_groups):
        v = v_ref[grp(g), :]
        vb = v.astype(BF16)
        o = _dot((q_ref[grp(g), :] * k_ref[grp(g), :]).astype(BF16), expand_ref[...]) * v
        o_ref[grp(g), :] = o + jnp.concatenate(
            [_dot(a_scr[g, h], vb[:, h * GLA_DV:(h + 1) * GLA_DV]) for h in range(GLA_HEADS)],
            axis=1)

    for c in range(n_groups * per_group):
        r = chunk_rows(c)
        bc = chunk_b(c)
        b_last = bc[GLA_CHUNK - 1:GLA_CHUNK, :]
        ke = (k_ref[r, :] * jnp.exp2(b_last - bc)).astype(BF16)
        vb = v_ref[r, :].astype(BF16)
        decay = jnp.exp2(jnp.broadcast_to(b_last, (GLA_DV, GK_W))).T
        for pair in range(n_pairs):
            tile = slice(pair * pair_w, (pair + 1) * pair_w)
            upd = []
            for h in (2 * pair, 2 * pair + 1):
                kv = _dot_tn(ke[:, tile], vb[:, h * GLA_DV:(h + 1) * GLA_DV])
                upd.append(kv[(h % 2) * GLA_DK:(h % 2 + 1) * GLA_DK])
            u_scr[c, pair] = jnp.concatenate(upd, axis=0)
            d_scr[c, pair] = decay[tile, :]

    for c in range(n_groups * per_group):
        for pair in range(n_pairs):
            tile = slice(pair * pair_w, (pair + 1) * pair_w)
            s_pair = s_scr[tile, :]
            sb_scr[c, pair] = s_pair.astype(BF16)
            s_scr[tile, :] = d_scr[c, pair] * s_pair + u_scr[c, pair]

    for c in range(n_groups * per_group):
        r = chunk_rows(c)
        qe = q_ref[r, :] * jnp.exp2(chunk_b(c))
        heads = []
        for h in range(GLA_HEADS):
            tile = slice((h // 2) * pair_w, (h // 2 + 1) * pair_w)
            heads.append(_dot(head_only(qe[:, tile], h).astype(BF16), sb_scr[c, h // 2]))
        o_ref[r, :] += jnp.concatenate(heads, axis=1)

    @pl.when(j == pl.num_programs(1) - 1)
    def _():
        sout_ref[...] = s_scr[...]


def _gla(q, k, v, la, s0, *, block):
    b, t, _ = q.shape
    assert t % block == 0 and block % GLA_GROUP == 0
    mexp, masks, expand = _gla_constants()
    n_groups = block // GLA_GROUP
    n_chunks = block // GLA_CHUNK
    n_pairs = GLA_HEADS // 2
    grid = (b, t // block)
    tok = lambda width: pl.BlockSpec((None, block, width), lambda i, j: (i, j, 0))
    st = pl.BlockSpec((None, GK_W, GLA_DV), lambda i, j: (i, 0, 0))
    return pl.pallas_call(
        functools.partial(_gla_kernel, n_groups=n_groups),
        grid=grid,
        in_specs=[tok(GK_W), tok(GK_W), tok(GV_W), tok(GK_W), st, _const_spec(mexp.shape),
                  _const_spec(masks.shape), _const_spec(expand.shape)],
        out_specs=[tok(GV_W), st],
        out_shape=[jax.ShapeDtypeStruct((b, t, GV_W), F32),
                   jax.ShapeDtypeStruct((b, GK_W, GLA_DV), F32)],
        scratch_shapes=[pltpu.VMEM((GK_W, GLA_DV), F32),
                        pltpu.VMEM((n_groups, (GLA_LEVELS + 1) * GLA_GROUP, GK_W), F32),
                        pltpu.VMEM((n_groups, GLA_HEADS, GLA_GROUP, GLA_GROUP), BF16),
                        pltpu.VMEM((n_chunks, n_pairs, 2 * GLA_DK, GLA_DV), F32),
                        pltpu.VMEM((n_chunks, n_pairs, 2 * GLA_DK, GLA_DV), F32),
                        pltpu.VMEM((n_chunks, n_pairs, 2 * GLA_DK, GLA_DV), BF16)],
        compiler_params=pltpu.CompilerParams(dimension_semantics=("arbitrary", "arbitrary"),
                                             vmem_limit_bytes=VMEM_LIMIT),
        name="gla",
    )(q, k, v, la, s0, mexp, masks, expand)


def _lambda(lp_ref, lam_init):
    lp = lp_ref[...]
    s1 = jnp.sum(lp[0:1, :] * lp[1:2, :], axis=1, keepdims=True)
    s2 = jnp.sum(lp[2:3, :] * lp[3:4, :], axis=1, keepdims=True)
    return jnp.exp(s1) - jnp.exp(s2) + lam_init


_RELAYOUT_ROWS = 512
ATTN_CHAIN = 256
ATTN_SLOTS = 32
ONES_ROWS = 16


def _attn_prompt_kernel(lp_ref, q_ref, kb_ref, vt_ref, o_ref, vt_scr, qst_scr, st_scr, m_scr,
                        acc_scr, *, tq, seq, lam_init):
    i = pl.program_id(2)
    cw = ATTN_CHAIN
    per_sub = tq // cw
    n_chains = 2 * per_sub

    @pl.when(i == 0)
    def _():
        vt_scr[:DIFF_DV, :] = vt_ref[...]
        vt_scr[DIFF_DV:, :] = jnp.ones((ONES_ROWS, seq), BF16)

    feat = lax.broadcasted_iota(jnp.int32, (2 * DIFF_DH, _RELAYOUT_ROWS), 0)
    first = feat < DIFF_DH
    for c in range(tq // _RELAYOUT_ROWS):
        r = slice(c * _RELAYOUT_ROWS, (c + 1) * _RELAYOUT_ROWS)
        qt = (q_ref[r, :] * (DIFF_DH ** -0.5 * math.log2(math.e))).T
        qst_scr[:, r] = jnp.where(first, qt, 0.0).astype(BF16)
        qst_scr[:, tq + c * _RELAYOUT_ROWS:tq + (c + 1) * _RELAYOUT_ROWS] = (
            jnp.where(first, 0.0, qt).astype(BF16))

    m_scr[...] = jnp.full(m_scr.shape, -jnp.inf, F32)
    acc_scr[...] = jnp.zeros(acc_scr.shape, F32)

    kpos = lax.broadcasted_iota(jnp.int32, (cw, cw), 0)
    qpos = lax.broadcasted_iota(jnp.int32, (cw, cw), 1)
    causal = kpos <= qpos

    def scores(slot, r, c):
        st_scr[:, slot * cw:(slot + 1) * cw] = _dot(kb_ref[r, :], qst_scr[:, c * cw:(c + 1) * cw])

    def softmax_pv(slot, r, c, masked):
        cols = slice(c * cw, (c + 1) * cw)
        st = st_scr[:, slot * cw:(slot + 1) * cw]
        if masked:
            st = jnp.where(causal, st, -jnp.inf)
        m_old = m_scr[:, cols]
        m_new = jnp.maximum(m_old, jnp.max(st, axis=0, keepdims=True))
        p = jnp.exp2(st - m_new).astype(BF16)
        acc_scr[:, cols] = jnp.exp2(m_old - m_new) * acc_scr[:, cols] + _dot(vt_scr[:, r], p)
        m_scr[:, cols] = m_new

    def run(items):
        assert len(items) <= ATTN_SLOTS
        for slot, (r, c, _) in enumerate(items):
            scores(slot, r, c)
        for slot, (r, c, masked) in enumerate(items):
            softmax_pv(slot, r, c, masked)

    blocks_per_trip = ATTN_SLOTS // n_chains

    def body(j, carry):
        items = []
        for u in range(blocks_per_trip):
            r = pl.ds(pl.multiple_of((j * blocks_per_trip + u) * cw, cw), cw)
            items += [(r, c, False) for c in range(n_chains)]
        run(items)
        return carry

    lax.fori_loop(0, i * per_sub // blocks_per_trip, body, 0)

    items = []
    for jj in range(per_sub):
        r = pl.ds(pl.multiple_of(i * tq + jj * cw, cw), cw)
        block = [(r, c, c % per_sub == jj) for c in range(n_chains) if c % per_sub >= jj]
        if len(items) + len(block) > ATTN_SLOTS:
            run(items)
            items = []
        items += block
    run(items)

    lam = _lambda(lp_ref, lam_init)
    for c in range(per_sub):
        c1 = slice(c * cw, (c + 1) * cw)
        c2 = slice(tq + c * cw, tq + (c + 1) * cw)
        o1 = acc_scr[:DIFF_DV, c1] / acc_scr[DIFF_DV:DIFF_DV + 1, c1]
        o2 = acc_scr[:DIFF_DV, c2] / acc_scr[DIFF_DV:DIFF_DV + 1, c2]
        o_ref[c1, :] = (o1 - lam * o2).T


def _attn_prompt(q, kb, vt, lam_params, lam_init, *, tq):
    b, s, _ = q.shape
    per_sub = tq // ATTN_CHAIN
    assert tq % ATTN_CHAIN == 0 and ATTN_SLOTS % (2 * per_sub) == 0
    assert per_sub % (ATTN_SLOTS // (2 * per_sub)) == 0
    grid = (b, DIFF_HEADS, s // tq)
    width = 2 * DIFF_DH
    qspec = pl.BlockSpec((None, tq, width), lambda bi, h, i: (bi, i, h))
    kspec = pl.BlockSpec((None, s, width), lambda bi, h, i: (bi, 0, h))
    vtspec = pl.BlockSpec((None, DIFF_DV, s), lambda bi, h, i: (bi, h, 0))
    return pl.pallas_call(
        functools.partial(_attn_prompt_kernel, tq=tq, seq=s, lam_init=lam_init),
        grid=grid,
        in_specs=[_const_spec((4, DIFF_DH)), qspec, kspec, vtspec],
        out_specs=qspec,
        out_shape=jax.ShapeDtypeStruct((b, s, DV_W), F32),
        scratch_shapes=[pltpu.VMEM((DIFF_DV + ONES_ROWS, s), BF16),
                        pltpu.VMEM((width, 2 * tq), BF16),
                        pltpu.VMEM((ATTN_CHAIN, ATTN_SLOTS * ATTN_CHAIN), F32),
                        pltpu.VMEM((1, 2 * tq), F32),
                        pltpu.VMEM((DIFF_DV + ONES_ROWS, 2 * tq), F32)],
        compiler_params=pltpu.CompilerParams(
            dimension_semantics=("arbitrary", "arbitrary", "arbitrary"),
            vmem_limit_bytes=VMEM_LIMIT),
        name="attn_prompt",
    )(lam_params, q, kb, vt)


def _head_norm(o, g):
    return jnp.concatenate(
        [_rms(o[:, h * LANES:(h + 1) * LANES], g) for h in range(o.shape[1] // LANES)], axis=1)


def _ffn_mix(x_ref, go_ref, gg_ref, do_ref, gn_ref, dn_ref, wout_ref, lam_init):
    gg = gg_ref[...]
    go = _head_norm(go_ref[...], gn_ref[...]) * (gg * jax.nn.sigmoid(gg))
    do = _head_norm(do_ref[...], dn_ref[...]) * (1.0 - lam_init)
    mix = jnp.concatenate([go, do], axis=1).astype(BF16)
    return x_ref[...] + _dot(mix, wout_ref[...])


def _ffn_gate_up(h, fn_ref, wg_ref, wu_ref):
    hn = _rms(h, fn_ref[...]).astype(BF16)
    return _dot(hn, wg_ref[...]), _dot(hn, wu_ref[...])


def _ffn_finish(h, gate, up, p_ref, wd_ref, pn_ref, wpg_ref, wpp_ref):
    ple = _dot(p_ref[...].astype(BF16), wpp_ref[...])
    h = h + _dot(((gate * jax.nn.sigmoid(gate)) * up).astype(BF16), wd_ref[...])
    pg = jax.nn.sigmoid(_dot(_rms(h, pn_ref[...]).astype(BF16), wpg_ref[...]))
    return h + pg * ple


def _attn_sample_ffn_kernel(pt_ref, lp_ref, q_ref, kn_ref, vn_ref, ckt_hbm, cv_hbm, x_ref, go_ref,
                            gg_ref, do_ref, p_ref, gn_ref, dn_ref, wout_ref, fn_ref, wg_ref, wu_ref,
                            wd_ref, pn_ref, wpg_ref, wpp_ref, o_ref, y_ref, qs_scr, m_scr, l_scr,
                            acc_scr, kbuf, vbuf, sem, *, pages, t_new, lam_init):
    p = pl.program_id(1)
    per_row = pl.num_programs(1)
    step = pl.program_id(0) * per_row + p
    last_step = pl.num_programs(0) * per_row - 1
    slot = step % 2
    n_rows = 2 * DIFF_HEADS * t_new
    head_rows = 2 * t_new

    def page_copies(of_step, into_slot):
        row = of_step // per_row
        col = (of_step % per_row) * pages
        copies = []
        for g in range(pages):
            page = pt_ref[row, col + g]
            copies.append(pltpu.make_async_copy(ckt_hbm.at[page], kbuf.at[into_slot, g],
                                                sem.at[into_slot, 0, g]))
            copies.append(pltpu.make_async_copy(cv_hbm.at[page], vbuf.at[into_slot, g],
                                                sem.at[into_slot, 1, g]))
        return copies

    @pl.when(step == 0)
    def _():
        for c in page_copies(0, 0):
            c.start()

    for c in page_copies(step, slot):
        c.wait()

    @pl.when(p == 0)
    def _():
        q = q_ref[...] * (DIFF_DH ** -0.5)
        lane = lax.broadcasted_iota(jnp.int32, (t_new, DQ_W), 1) // DIFF_DH
        qs = jnp.concatenate([jnp.where(lane == g, q, 0.0) for g in range(2 * DIFF_HEADS)],
                             axis=0).astype(BF16)
        qs_scr[...] = qs
        s = _dot_nt(qs, kn_ref[...].astype(BF16))
        qpos = lax.broadcasted_iota(jnp.int32, (n_rows, t_new), 0) % t_new
        kpos = lax.broadcasted_iota(jnp.int32, (n_rows, t_new), 1)
        s = jnp.where(kpos <= qpos, s, -jnp.inf)
        m = jnp.max(s, axis=1, keepdims=True)
        e = jnp.exp(s - m)
        m_scr[...] = m
        l_scr[...] = jnp.sum(e, axis=1, keepdims=True)
        eb = e.astype(BF16)
        vn = vn_ref[...].astype(BF16)
        acc_scr[...] = jnp.concatenate(
            [_dot(eb[h * head_rows:(h + 1) * head_rows, :], vn[:, h * DIFF_DV:(h + 1) * DIFF_DV])
             for h in range(DIFF_HEADS)], axis=0)

    h_mid = _ffn_mix(x_ref, go_ref, gg_ref, do_ref, gn_ref, dn_ref, wout_ref, lam_init)
    qs = qs_scr[...]
    s = jnp.concatenate(
        [_dot(qs, jnp.concatenate([kbuf[slot, g], kbuf[slot, g + 1]], axis=1).astype(BF16))
         for g in range(0, pages, 2)], axis=1)
    for c in page_copies(jnp.minimum(step + 1, last_step), 1 - slot):
        c.start()
    gate, up = _ffn_gate_up(h_mid, fn_ref, wg_ref, wu_ref)
    m_old = m_scr[...]
    m_new = jnp.maximum(m_old, jnp.max(s, axis=1, keepdims=True))
    alpha = jnp.exp(m_old - m_new)
    e = jnp.exp(s - m_new)
    l_scr[...] = alpha * l_scr[...] + jnp.sum(e, axis=1, keepdims=True)
    eb = e.astype(BF16)
    pv = []
    for h in range(DIFF_HEADS):
        rows = slice(h * head_rows, (h + 1) * head_rows)
        acc_h = None
        for g in range(0, pages, 2):
            v_h = jnp.concatenate(
                [vbuf[slot, g, pl.ds(h, PAGE_SIZE, stride=DIFF_HEADS), :],
                 vbuf[slot, g + 1, pl.ds(h, PAGE_SIZE, stride=DIFF_HEADS), :]],
                axis=0).astype(BF16)
            t = _dot(eb[rows, g * PAGE_SIZE:(g + 2) * PAGE_SIZE], v_h)
            acc_h = t if acc_h is None else acc_h + t
        pv.append(acc_h)
    y_ref[...] = _ffn_finish(h_mid, gate, up, p_ref, wd_ref, pn_ref, wpg_ref, wpp_ref)
    acc_scr[...] = alpha * acc_scr[...] + jnp.concatenate(pv, axis=0)
    m_scr[...] = m_new

    @pl.when(p == pl.num_programs(1) - 1)
    def _():
        out = acc_scr[...] / l_scr[...]
        lam = _lambda(lp_ref, lam_init)
        heads = []
        for h in range(DIFF_HEADS):
            r1 = h * head_rows
            r2 = r1 + t_new
            heads.append(out[r1:r1 + t_new, :] - lam * out[r2:r2 + t_new, :])
        o_ref[...] = jnp.concatenate(heads, axis=1)

    @pl.when(step == last_step)
    def _():
        for c in page_copies(last_step, 1 - slot):
            c.wait()


def _ffn_operands(x, go, gg, do, p, w):
    return (x, go, gg, do, p, w["gla_norm_g"], w["diff_norm_g"], w["w_out"], w["ffn_norm_g"],
            w["w_ffn_gate"], w["w_ffn_up"], w["w_ffn_down"], w["ple_norm_g"], w["w_ple_gate"],
            w["w_ple_proj"])


def _ffn_in_specs(tok, ple, d_ff):
    return [tok(D_MODEL), tok(GV_W), tok(GV_W), tok(DV_W), tok(ple),
            _const_spec((1, GLA_DV)), _const_spec((1, DIFF_DV)),
            _const_spec((D_MODEL, D_MODEL)), _const_spec((1, D_MODEL)),
            _const_spec((D_MODEL, d_ff)), _const_spec((D_MODEL, d_ff)),
            _const_spec((d_ff, D_MODEL)), _const_spec((1, D_MODEL)),
            _const_spec((D_MODEL, D_MODEL)), _const_spec((ple, D_MODEL))]


def _attn_sample_ffn(q, k_new, v_new, cache_kt, cache_v, page_table, lam_params, lam_init, ffn, *,
                     pages):
    db, t_new, _ = q.shape
    n_pages = page_table.shape[1]
    per_row = n_pages // pages
    assert per_row * pages == n_pages and pages % 2 == 0
    grid = (db, per_row)
    n_rows = 2 * DIFF_HEADS * t_new
    x, w = ffn[0], ffn[5]
    n = x.shape[0]
    tm = n // (db * per_row)
    assert tm * db * per_row == n and tm % 8 == 0
    tok = pl.BlockSpec((None, t_new, DQ_W), lambda b, p, pt: (b, 0, 0))
    ftok = lambda width: pl.BlockSpec((tm, width), lambda b, p, pt: (b * per_row + p, 0))
    hbm = pl.BlockSpec(memory_space=pl.ANY)
    grid_spec = pltpu.PrefetchScalarGridSpec(
        num_scalar_prefetch=1,
        grid=grid,
        in_specs=[pl.BlockSpec((4, DIFF_DH), lambda b, p, pt: (0, 0)), tok, tok, tok, hbm, hbm]
        + _ffn_in_specs(ftok, ffn[4].shape[1], w["w_ffn_gate"].shape[1]),
        out_specs=[tok, ftok(D_MODEL)],
        scratch_shapes=[pltpu.VMEM((n_rows, DQ_W), BF16), pltpu.VMEM((n_rows, 1), F32),
                        pltpu.VMEM((n_rows, 1), F32), pltpu.VMEM((n_rows, DIFF_DV), F32),
                        pltpu.VMEM((2, pages, DQ_W, PAGE_SIZE), F32),
                        pltpu.VMEM((2, pages, PAGE_SIZE * DIFF_HEADS, DIFF_DV), F32),
                        pltpu.SemaphoreType.DMA((2, 2, pages))],
    )
    return pl.pallas_call(
        functools.partial(_attn_sample_ffn_kernel, pages=pages, t_new=t_new, lam_init=lam_init),
        grid_spec=grid_spec,
        out_shape=[jax.ShapeDtypeStruct((db, t_new, DV_W), F32),
                   jax.ShapeDtypeStruct((n, D_MODEL), F32)],
        compiler_params=pltpu.CompilerParams(dimension_semantics=("arbitrary", "arbitrary"),
                                             vmem_limit_bytes=VMEM_LIMIT),
        name="attn_sample_ffn",
    )(page_table, lam_params, q, k_new, v_new, cache_kt, cache_v, *_ffn_operands(*ffn))


def _out_ffn_kernel(x_ref, go_ref, gg_ref, do_ref, p_ref, gn_ref, dn_ref, wout_ref, fn_ref,
                    wg_ref, wu_ref, wd_ref, pn_ref, wpg_ref, wpp_ref, y_ref, *, lam_init):
    h = _ffn_mix(x_ref, go_ref, gg_ref, do_ref, gn_ref, dn_ref, wout_ref, lam_init)
    gate, up = _ffn_gate_up(h, fn_ref, wg_ref, wu_ref)
    y_ref[...] = _ffn_finish(h, gate, up, p_ref, wd_ref, pn_ref, wpg_ref, wpp_ref)


def _out_ffn(x, go, gg, do, p, w, lam_init, *, tm):
    n = x.shape[0]
    tok = lambda width: pl.BlockSpec((tm, width), lambda i: (i, 0))
    return pl.pallas_call(
        functools.partial(_out_ffn_kernel, lam_init=lam_init),
        grid=(n // tm,),
        in_specs=_ffn_in_specs(tok, p.shape[1], w["w_ffn_gate"].shape[1]),
        out_specs=tok(D_MODEL),
        out_shape=jax.ShapeDtypeStruct((n, D_MODEL), F32),
        compiler_params=pltpu.CompilerParams(dimension_semantics=("arbitrary",),
                                             vmem_limit_bytes=VMEM_LIMIT),
        name="out_ffn",
    )(*_ffn_operands(x, go, gg, do, p, w))


def _rope_tables(pos):
    half = DIFF_DH // 2
    inv = ROPE_THETA ** (-jnp.arange(half, dtype=F32) / half)
    ang = pos.astype(F32)[:, None] * inv[None, :]
    cos, sin = jnp.cos(ang), jnp.sin(ang)
    cos = jnp.concatenate([cos, cos], axis=1)
    sin = jnp.concatenate([-sin, sin], axis=1)
    reps = LANES // DIFF_DH
    return jnp.tile(cos, (1, reps)), jnp.tile(sin, (1, reps))


def _layer_weights(i, attn_norm_g, w_in, w_gk2, b_gk, q_norm_g, k_norm_g, lam_params, gla_norm_g,
                   diff_norm_g, w_out, ffn_norm_g, w_ffn_gate, w_ffn_up, w_ffn_down, ple_norm_g,
                   w_ple_gate, w_ple_proj):
    lr0 = 2 * GK_W + 2 * GV_W
    wi = w_in[i]
    w_lr = jnp.pad(wi[:, lr0:lr0 + GLA_GATE_RANK], ((0, 0), (0, LR_PAD - GLA_GATE_RANK)))
    gid = jnp.arange(LANES) // DIFF_DH
    dq0 = lr0 + GLA_GATE_RANK
    row = lambda a: a.reshape(1, -1).astype(F32)
    return {
        "attn_norm_g": row(attn_norm_g[i]),
        "w_proj": jnp.concatenate([wi[:, dq0:dq0 + 2 * DQ_W], wi[:, :lr0], wi[:, dq0 + 2 * DQ_W:]],
                                  axis=1).astype(BF16),
        "w_lr": w_lr.astype(BF16),
        "w_gk2": jnp.pad(w_gk2[i], ((0, LR_PAD - GLA_GATE_RANK), (0, 0))).astype(BF16),
        "b_gk": row(b_gk[i]),
        "q_norm_g": row(jnp.tile(q_norm_g[i], DQ_W // DIFF_DH)),
        "k_norm_g": row(jnp.tile(k_norm_g[i], DQ_W // DIFF_DH)),
        "group_ind": (gid[:, None] == gid[None, :]).astype(BF16),
        "lam_params": lam_params[i].astype(F32),
        "gla_norm_g": row(gla_norm_g[i]),
        "diff_norm_g": row(diff_norm_g[i]),
        "w_out": w_out[i].astype(BF16),
        "ffn_norm_g": row(ffn_norm_g[i]),
        "w_ffn_gate": w_ffn_gate[i].astype(BF16),
        "w_ffn_up": w_ffn_up[i].astype(BF16),
        "w_ffn_down": w_ffn_down[i].astype(BF16),
        "ple_norm_g": row(ple_norm_g[i]),
        "w_ple_gate": w_ple_gate[i].astype(BF16),
        "w_ple_proj": w_ple_proj[i].astype(BF16),
    }


def kernel(x_prompt, x_sample, p_prompt, p_sample, cache_k, cache_v, state_gla, page_table, attn_norm_g, w_in, w_gk2, b_gk, q_norm_g, k_norm_g, lam_params, gla_norm_g, diff_norm_g, w_out, ffn_norm_g, w_ffn_gate, w_ffn_up, w_ffn_down, ple_norm_g, w_ple_gate, w_ple_proj):
    B, S, _ = x_prompt.shape
    DB, T, _ = x_sample.shape
    depth = w_in.shape[0]
    n_pool = cache_k.shape[1]
    n_pages = page_table.shape[1]
    past_len = n_pages * PAGE_SIZE

    cos_p, sin_p = _rope_tables(jnp.arange(S))
    cos_s, sin_s = _rope_tables(jnp.tile(past_len + jnp.arange(T), DB))
    t_pad = GLA_GROUP
    cache_k2 = jnp.transpose(cache_k, (0, 1, 3, 4, 5, 2)).reshape(depth * n_pool, DQ_W, PAGE_SIZE)
    cache_v2 = cache_v.reshape(depth * n_pool, PAGE_SIZE * DIFF_HEADS, DIFF_DV)

    hp = x_prompt.reshape(B * S, D_MODEL)
    hs = x_sample.reshape(DB * T, D_MODEL)
    kp_l, vp_l, gp_l, ks_l, vs_l, gs_l = [], [], [], [], [], []
    for i in range(depth):
        lam_init = 0.8 - 0.6 * math.exp(-0.3 * i)
        w = _layer_weights(i, attn_norm_g, w_in, w_gk2, b_gk, q_norm_g, k_norm_g, lam_params,
                           gla_norm_g, diff_norm_g, w_out, ffn_norm_g, w_ffn_gate, w_ffn_up,
                           w_ffn_down, ple_norm_g, w_ple_gate, w_ple_proj)

        gq, gk, gv, gg, la, dq, dkb, dkt, dv4, dvt = _proj(hp, cos_p, sin_p, w, tm=PROJ_TM,
                                                           pos_period_blocks=S // PROJ_TM, seq=S)
        r3 = lambda a: a.reshape(B, S, a.shape[-1])
        s0 = jnp.zeros((B, GK_W, GLA_DV), F32)
        gla_o, gla_s = _gla(r3(gq), r3(gk), r3(gv), r3(la), s0, block=GLA_BLOCK)
        diff_o = _attn_prompt(r3(dq), r3(dkb), dvt, w["lam_params"], lam_init, tq=ATTN_TQ)
        ffn_prompt = (hp, gla_o.reshape(B * S, GV_W), gg, diff_o.reshape(B * S, DV_W),
                      p_prompt[i].reshape(B * S, -1), w)
        kp_l.append(jnp.transpose(dkt.reshape(B, DIFF_HEADS, 2, DIFF_DH, S), (0, 4, 1, 2, 3)))
        vp_l.append(dv4.reshape(B, S, DIFF_HEADS, DIFF_DV))
        gp_l.append(gla_s.reshape(B, GLA_HEADS, GLA_DK, GLA_DV))

        gq, gk, gv, gg, la, dq, dk, dv = _proj(hs, cos_s, sin_s, w, tm=DB * T, pos_period_blocks=1)
        r3 = lambda a: a.reshape(DB, T, a.shape[-1])
        padt = lambda a: jnp.pad(r3(a), ((0, 0), (0, t_pad - T), (0, 0)))
        s0 = state_gla[i].reshape(DB, GK_W, GLA_DV)
        gla_o, gla_s = _gla(padt(gq), padt(gk), padt(gv), padt(la), s0, block=t_pad)
        gla_o = gla_o[:, :T]
        diff_o, hp = _attn_sample_ffn(r3(dq), r3(dk), r3(dv), cache_k2, cache_v2,
                                      page_table + i * n_pool, w["lam_params"], lam_init,
                                      ffn_prompt, pages=PAGES_PER_STEP)
        hs = _out_ffn(hs, gla_o.reshape(DB * T, GV_W), gg, diff_o.reshape(DB * T, DV_W),
                      p_sample[i].reshape(DB * T, -1), w, lam_init, tm=DB * T)
        ks_l.append(dk.reshape(DB, T, DIFF_HEADS, 2, DIFF_DH))
        vs_l.append(dv.reshape(DB, T, DIFF_HEADS, DIFF_DV))
        gs_l.append(gla_s.reshape(DB, GLA_HEADS, GLA_DK, GLA_DV))

    return (hp.reshape(B, S, D_MODEL), hs.reshape(DB, T, D_MODEL),
            jnp.stack(kp_l), jnp.stack(vp_l), jnp.stack(gp_l),
            jnp.stack(ks_l), jnp.stack(vs_l), jnp.stack(gs_l))
```

```python
import functools
import math

import jax
import jax.numpy as jnp
import numpy as np
from jax import lax
from jax.experimental import pallas as pl
from jax.experimental.pallas import tpu as pltpu

F32 = jnp.float32
BF16 = jnp.bfloat16

D_MODEL = 1024
GLA_HEADS = 4
GLA_DV = 128
GLA_DK = 64
GLA_GATE_RANK = 16
GLA_GATE_NORMALIZER = 16.0
GLA_CHUNK = 64
DIFF_HEADS = 4
DIFF_DV = 128
DIFF_DH = 64
ROPE_THETA = 10000.0
PAGE_SIZE = 128
EPS = 1e-6
GK_W = GLA_HEADS * GLA_DK
GV_W = GLA_HEADS * GLA_DV
DQ_W = DIFF_HEADS * 2 * DIFF_DH
DV_W = DIFF_HEADS * DIFF_DV
LANES = 128
LR_PAD = LANES
PROJ_W = 2 * GK_W + 2 * GV_W + 2 * DQ_W + DV_W
VMEM_LIMIT = 56 * 1024 * 1024

PROJ_TM = 512
GLA_BLOCK = 512
ATTN_TQ = 1024
PAGES_PER_STEP = 16


def _dot(a, b):
    return jnp.dot(a, b, preferred_element_type=F32)


def _dot_nt(a, b):
    return lax.dot_general(a, b, (((1,), (1,)), ((), ())), preferred_element_type=F32)


def _dot_tn(a, b):
    return lax.dot_general(a, b, (((0,), (0,)), ((), ())), preferred_element_type=F32)


def _rms(x, g):
    return x * lax.rsqrt(jnp.mean(x * x, axis=-1, keepdims=True) + EPS) * g


def _const_spec(shape):
    nd = len(shape)
    return pl.BlockSpec(shape, lambda *_: (0,) * nd, pipeline_mode=pl.Buffered(1))


def _proj_kernel(x_ref, ng_ref, w_ref, wlr_ref, wgk2_ref, bgk_ref, qg_ref, kg_ref, cos_ref,
                 sin_ref, ind_ref, gq_ref, gk_ref, gv_ref, gg_ref, la_ref, dq_ref, *kv_refs,
                 feature_major):
    x = x_ref[...]
    xn = _rms(x, ng_ref[...]).astype(BF16)
    qk = _dot(xn, w_ref[:, :2 * DQ_W])
    dq = qk[:, :DQ_W]
    dk = qk[:, DQ_W:]

    tm = x.shape[0]
    cos = jnp.concatenate([cos_ref[...]] * (DQ_W // LANES), axis=1)
    sin = jnp.concatenate([sin_ref[...]] * (DQ_W // LANES), axis=1)
    lane = lax.broadcasted_iota(jnp.int32, (tm, DQ_W), 1)
    upper = (lane & (DIFF_DH // 2)) != 0
    ind = ind_ref[...]

    def norm_rope(y, g):
        ms = jnp.concatenate(
            [_dot((y[:, t * LANES:(t + 1) * LANES] ** 2).astype(BF16), ind)
             for t in range(DQ_W // LANES)], axis=1) * (1.0 / DIFF_DH)
        yn = y * lax.rsqrt(ms + EPS) * g
        partner = jnp.where(upper, pltpu.roll(yn, DIFF_DH // 2, 1),
                            pltpu.roll(yn, DQ_W - DIFF_DH // 2, 1))
        return yn * cos + partner * sin

    dq_ref[...] = norm_rope(dq, qg_ref[...])
    dv = _dot(xn, w_ref[:, PROJ_W - DV_W:])
    dk = norm_rope(dk, kg_ref[...])
    if feature_major:
        dkb_ref, dkt_ref, dv4_ref, dvt_ref = kv_refs
        dkb_ref[...] = dk.astype(BF16)
        dkt_ref[...] = dk.T
        for h in range(DIFF_HEADS):
            dv4_ref[pl.ds(h, tm, stride=DIFF_HEADS), :] = dv[:, h * DIFF_DV:(h + 1) * DIFF_DV]
        dvt_ref[...] = dv.T.astype(BF16)
    else:
        dk_ref, dv_ref = kv_refs
        dk_ref[...] = dk
        dv_ref[...] = dv

    glr = _dot(xn, wlr_ref[...])
    z = _dot(glr.astype(BF16), wgk2_ref[...]) + bgk_ref[...]
    logsig = jnp.minimum(z, 0.0) - jnp.log(1.0 + jnp.exp(-jnp.abs(z)))
    la_ref[...] = logsig * (1.0 / GLA_GATE_NORMALIZER)

    gla = _dot(xn, w_ref[:, 2 * DQ_W:PROJ_W - DV_W])
    o = 0
    gq_ref[...] = gla[:, o:o + GK_W] * (GLA_DK ** -0.5); o += GK_W
    gk_ref[...] = gla[:, o:o + GK_W]; o += GK_W
    gv_ref[...] = gla[:, o:o + GV_W]; o += GV_W
    gg_ref[...] = gla[:, o:o + GV_W]


def _proj(x, pos_cos, pos_sin, w, *, tm, pos_period_blocks, seq=None):
    n = x.shape[0]
    grid = (n // tm,)
    tok = lambda width: pl.BlockSpec((tm, width), lambda i: (i, 0))
    pos = pl.BlockSpec((tm, LANES), lambda i: (i % pos_period_blocks, 0))
    out_widths = (GK_W, GK_W, GV_W, GV_W, GK_W, DQ_W)
    out_specs = [tok(wd) for wd in out_widths]
    out_shape = [jax.ShapeDtypeStruct((n, wd), F32) for wd in out_widths]
    if seq is None:
        out_specs += [tok(DQ_W), tok(DV_W)]
        out_shape += [jax.ShapeDtypeStruct((n, DQ_W), F32), jax.ShapeDtypeStruct((n, DV_W), F32)]
    else:
        per_seq = seq // tm
        fm = lambda width: pl.BlockSpec((None, width, tm), lambda i: (i // per_seq, 0, i % per_seq))
        out_specs += [tok(DQ_W), fm(DQ_W),
                      pl.BlockSpec((tm * DIFF_HEADS, DIFF_DV), lambda i: (i, 0)), fm(DV_W)]
        out_shape += [jax.ShapeDtypeStruct((n, DQ_W), BF16),
                      jax.ShapeDtypeStruct((n // seq, DQ_W, seq), F32),
                      jax.ShapeDtypeStruct((n * DIFF_HEADS, DIFF_DV), F32),
                      jax.ShapeDtypeStruct((n // seq, DV_W, seq), BF16)]
    return pl.pallas_call(
        functools.partial(_proj_kernel, feature_major=seq is not None),
        grid=grid,
        in_specs=[tok(D_MODEL), _const_spec((1, D_MODEL)), _const_spec((D_MODEL, PROJ_W)),
                  _const_spec((D_MODEL, LR_PAD)), _const_spec((LR_PAD, GK_W)),
                  _const_spec((1, GK_W)), _const_spec((1, DQ_W)), _const_spec((1, DQ_W)),
                  pos, pos, _const_spec((LANES, LANES))],
        out_specs=out_specs,
        out_shape=out_shape,
        compiler_params=pltpu.CompilerParams(dimension_semantics=("arbitrary",),
                                             vmem_limit_bytes=VMEM_LIMIT),
        name="proj",
    )(x, w["attn_norm_g"], w["w_proj"], w["w_lr"], w["w_gk2"], w["b_gk"], w["q_norm_g"],
      w["k_norm_g"], pos_cos, pos_sin, w["group_ind"])


GLA_GROUP = 128
GLA_LEVELS = GLA_CHUNK.bit_length() - 1


def _gla_constants():
    n = GLA_GROUP
    t = np.arange(n)[:, None]
    u = np.arange(n)[None, :]
    mats = [((t // GLA_CHUNK == u // GLA_CHUNK) & (u <= t)).astype(np.float32)]
    masks = []
    for lvl in range(GLA_LEVELS):
        h = 1 << lvl
        ref = (t // (2 * h)) * (2 * h) + h - 1
        upper = (t & h) != 0
        mats.append(np.where(upper & (u > ref) & (u <= t), 1.0, 0.0)
                    - np.where(~upper & (u > t) & (u <= ref), 1.0, 0.0))
        masks.append(((t // (2 * h) == u // (2 * h)) & upper & ((u & h) == 0)).astype(np.float32))
    head = np.arange(GK_W)[:, None] // GLA_DK == np.arange(GV_W)[None, :] // GLA_DV
    return (jnp.asarray(np.concatenate(mats, 0), BF16), jnp.asarray(np.concatenate(masks, 0), BF16),
            jnp.asarray(head.astype(np.float32), BF16))


def _gla_kernel(q_ref, k_ref, v_ref, la_ref, s0_ref, mexp_ref, mask_ref, expand_ref, o_ref,
                sout_ref, s_scr, ex_scr, a_scr, u_scr, d_scr, sb_scr, *, n_groups):
    j = pl.program_id(1)
    n = GLA_GROUP
    pair_w = 2 * GLA_DK
    n_pairs = GLA_HEADS // 2
    per_group = n // GLA_CHUNK

    @pl.when(j == 0)
    def _():
        s_scr[...] = s0_ref[...]

    def head_only(x, h):
        lane = lax.broadcasted_iota(jnp.int32, x.shape, 1)
        keep = lane < GLA_DK if h % 2 == 0 else lane >= GLA_DK
        return jnp.where(keep, x, 0.0)

    grp = lambda g: slice(g * n, (g + 1) * n)
    chunk_rows = lambda c: slice(c * GLA_CHUNK, (c + 1) * GLA_CHUNK)

    def chunk_b(c):
        lo = (c % per_group) * GLA_CHUNK
        return ex_scr[c // per_group, lo:lo + GLA_CHUNK, :]

    for g in range(n_groups):
        la = la_ref[grp(g), :] * math.log2(math.e)
        la_hi = la.astype(BF16)
        la_lo = (la - la_hi.astype(F32)).astype(BF16)
        ex_scr[g, :n, :] = _dot(mexp_ref[:n, :], la_hi) + _dot(mexp_ref[:n, :], la_lo)
        ex_scr[g, n:, :] = _dot(mexp_ref[n:, :], la_hi)

    for g in range(n_groups):
        q = q_ref[grp(g), :]
        k = k_ref[grp(g), :]
        tiles = [slice((h // 2) * pair_w, (h // 2 + 1) * pair_w) for h in range(GLA_HEADS)]
        a = [0.0] * GLA_HEADS
        for lvl in range(GLA_LEVELS):
            d = ex_scr[g, (lvl + 1) * n:(lvl + 2) * n, :]
            ql = q * jnp.exp2(jnp.minimum(d, 0.0))
            kl = (k * jnp.exp2(jnp.minimum(-d, 0.0))).astype(BF16)
            msk = mask_ref[lvl * n:(lvl + 1) * n, :] > 0
            for h in range(GLA_HEADS):
                t = _dot_nt(head_only(ql[:, tiles[h]], h).astype(BF16), kl[:, tiles[h]])
                a[h] = jnp.where(msk, t, a[h])
        for h in range(GLA_HEADS):
            a_scr[g, h] = a[h].astype(BF16)

    for g in range(n_groups):
        v = v_ref[grp(g), :]
        vb = v.astype(BF16)
        o = _dot((q_ref[grp(g), :] * k_ref[grp(g), :]).astype(BF16), expand_ref[...]) * v
        o_ref[grp(g), :] = o + jnp.concatenate(
            [_dot(a_scr[g, h], vb[:, h * GLA_DV:(h + 1) * GLA_DV]) for h in range(GLA_HEADS)],
            axis=1)

    for c in range(n_groups * per_group):
        r = chunk_rows(c)
        bc = chunk_b(c)
        b_last = bc[GLA_CHUNK - 1:GLA_CHUNK, :]
        ke = (k_ref[r, :] * jnp.exp2(b_last - bc)).astype(BF16)
        vb = v_ref[r, :].astype(BF16)
        decay = jnp.exp2(jnp.broadcast_to(b_last, (GLA_DV, GK_W))).T
        for pair in range(n_pairs):
            tile = slice(pair * pair_w, (pair + 1) * pair_w)
            upd = []
            for h in (2 * pair, 2 * pair + 1):
                kv = _dot_tn(ke[:, tile], vb[:, h * GLA_DV:(h + 1) * GLA_DV])
                upd.append(kv[(h % 2) * GLA_DK:(h % 2 + 1) * GLA_DK])
            u_scr[c, pair] = jnp.concatenate(upd, axis=0)
            d_scr[c, pair] = decay[tile, :]

    for c in range(n_groups * per_group):
        for pair in range(n_pairs):
            tile = slice(pair * pair_w, (pair + 1) * pair_w)
            s_pair = s_scr[tile, :]
            sb_scr[c, pair] = s_pair.astype(BF16)
            s_scr[tile, :] = d_scr[c, pair] * s_pair + u_scr[c, pair]

    for c in range(n_groups * per_group):
        r = chunk_rows(c)
        qe = q_ref[r, :] * jnp.exp2(chunk_b(c))
        heads = []
        for h in range(GLA_HEADS):
            tile = slice((h // 2) * pair_w, (h // 2 + 1) * pair_w)
            heads.append(_dot(head_only(qe[:, tile], h).astype(BF16), sb_scr[c, h // 2]))
        o_ref[r, :] += jnp.concatenate(heads, axis=1)

    @pl.when(j == pl.num_programs(1) - 1)
    def _():
        sout_ref[...] = s_scr[...]


def _gla(q, k, v, la, s0, *, block):
    b, t, _ = q.shape
    assert t % block == 0 and block % GLA_GROUP == 0
    mexp, masks, expand = _gla_constants()
    n_groups = block // GLA_GROUP
    n_chunks = block // GLA_CHUNK
    n_pairs = GLA_HEADS // 2
    grid = (b, t // block)
    tok = lambda width: pl.BlockSpec((None, block, width), lambda i, j: (i, j, 0))
    st = pl.BlockSpec((None, GK_W, GLA_DV), lambda i, j: (i, 0, 0))
    return pl.pallas_call(
        functools.partial(_gla_kernel, n_groups=n_groups),
        grid=grid,
        in_specs=[tok(GK_W), tok(GK_W), tok(GV_W), tok(GK_W), st, _const_spec(mexp.shape),
                  _const_spec(masks.shape), _const_spec(expand.shape)],
        out_specs=[tok(GV_W), st],
        out_shape=[jax.ShapeDtypeStruct((b, t, GV_W), F32),
                   jax.ShapeDtypeStruct((b, GK_W, GLA_DV), F32)],
        scratch_shapes=[pltpu.VMEM((GK_W, GLA_DV), F32),
                        pltpu.VMEM((n_groups, (GLA_LEVELS + 1) * GLA_GROUP, GK_W), F32),
                        pltpu.VMEM((n_groups, GLA_HEADS, GLA_GROUP, GLA_GROUP), BF16),
                        pltpu.VMEM((n_chunks, n_pairs, 2 * GLA_DK, GLA_DV), F32),
                        pltpu.VMEM((n_chunks, n_pairs, 2 * GLA_DK, GLA_DV), F32),
                        pltpu.VMEM((n_chunks, n_pairs, 2 * GLA_DK, GLA_DV), BF16)],
        compiler_params=pltpu.CompilerParams(dimension_semantics=("arbitrary", "arbitrary"),
                                             vmem_limit_bytes=VMEM_LIMIT),
        name="gla",
    )(q, k, v, la, s0, mexp, masks, expand)


def _lambda(lp_ref, lam_init):
    lp = lp_ref[...]
    s1 = jnp.sum(lp[0:1, :] * lp[1:2, :], axis=1, keepdims=True)
    s2 = jnp.sum(lp[2:3, :] * lp[3:4, :], axis=1, keepdims=True)
    return jnp.exp(s1) - jnp.exp(s2) + lam_init


_RELAYOUT_ROWS = 512
ATTN_CHAIN = 256
ATTN_SLOTS = 32
ONES_ROWS = 16


def _attn_prompt_kernel(lp_ref, q_ref, kb_ref, vt_ref, o_ref, vt_scr, qst_scr, st_scr, m_scr,
                        acc_scr, *, tq, seq, lam_init):
    i = pl.program_id(2)
    cw = ATTN_CHAIN
    per_sub = tq // cw
    n_chains = 2 * per_sub

    @pl.when(i == 0)
    def _():
        vt_scr[:DIFF_DV, :] = vt_ref[...]
        vt_scr[DIFF_DV:, :] = jnp.ones((ONES_ROWS, seq), BF16)

    feat = lax.broadcasted_iota(jnp.int32, (2 * DIFF_DH, _RELAYOUT_ROWS), 0)
    first = feat < DIFF_DH
    for c in range(tq // _RELAYOUT_ROWS):
        r = slice(c * _RELAYOUT_ROWS, (c + 1) * _RELAYOUT_ROWS)
        qt = (q_ref[r, :] * (DIFF_DH ** -0.5 * math.log2(math.e))).T
        qst_scr[:, r] = jnp.where(first, qt, 0.0).astype(BF16)
        qst_scr[:, tq + c * _RELAYOUT_ROWS:tq + (c + 1) * _RELAYOUT_ROWS] = (
            jnp.where(first, 0.0, qt).astype(BF16))

    m_scr[...] = jnp.full(m_scr.shape, -jnp.inf, F32)
    acc_scr[...] = jnp.zeros(acc_scr.shape, F32)

    kpos = lax.broadcasted_iota(jnp.int32, (cw, cw), 0)
    qpos = lax.broadcasted_iota(jnp.int32, (cw, cw), 1)
    causal = kpos <= qpos

    def scores(slot, r, c):
        st_scr[:, slot * cw:(slot + 1) * cw] = _dot(kb_ref[r, :], qst_scr[:, c * cw:(c + 1) * cw])

    def softmax_pv(slot, r, c, masked):
        cols = slice(c * cw, (c + 1) * cw)
        st = st_scr[:, slot * cw:(slot + 1) * cw]
        if masked:
            st = jnp.where(causal, st, -jnp.inf)
        m_old = m_scr[:, cols]
        m_new = jnp.maximum(m_old, jnp.max(st, axis=0, keepdims=True))
        p = jnp.exp2(st - m_new).astype(BF16)
        acc_scr[:, cols] = jnp.exp2(m_old - m_new) * acc_scr[:, cols] + _dot(vt_scr[:, r], p)
        m_scr[:, cols] = m_new

    def run(items):
        assert len(items) <= ATTN_SLOTS
        for slot, (r, c, _) in enumerate(items):
            scores(slot, r, c)
        for slot, (r, c, masked) in enumerate(items):
            softmax_pv(slot, r, c, masked)

    blocks_per_trip = ATTN_SLOTS // n_chains

    def body(j, carry):
        items = []
        for u in range(blocks_per_trip):
            r = pl.ds(pl.multiple_of((j * blocks_per_trip + u) * cw, cw), cw)
            items += [(r, c, False) for c in range(n_chains)]
        run(items)
        return carry

    lax.fori_loop(0, i * per_sub // blocks_per_trip, body, 0)

    items = []
    for jj in range(per_sub):
        r = pl.ds(pl.multiple_of(i * tq + jj * cw, cw), cw)
        block = [(r, c, c % per_sub == jj) for c in range(n_chains) if c % per_sub >= jj]
        if len(items) + len(block) > ATTN_SLOTS:
            run(items)
            items = []
        items += block
    run(items)

    lam = _lambda(lp_ref, lam_init)
    for c in range(per_sub):
        c1 = slice(c * cw, (c + 1) * cw)
        c2 = slice(tq + c * cw, tq + (c + 1) * cw)
        o1 = acc_scr[:DIFF_DV, c1] / acc_scr[DIFF_DV:DIFF_DV + 1, c1]
        o2 = acc_scr[:DIFF_DV, c2] / acc_scr[DIFF_DV:DIFF_DV + 1, c2]
        o_ref[c1, :] = (o1 - lam * o2).T


def _attn_prompt(q, kb, vt, lam_params, lam_init, *, tq):
    b, s, _ = q.shape
    per_sub = tq // ATTN_CHAIN
    assert tq % ATTN_CHAIN == 0 and ATTN_SLOTS % (2 * per_sub) == 0
    assert per_sub % (ATTN_SLOTS // (2 * per_sub)) == 0
    grid = (b, DIFF_HEADS, s // tq)
    width = 2 * DIFF_DH
    qspec = pl.BlockSpec((None, tq, width), lambda bi, h, i: (bi, i, h))
    kspec = pl.BlockSpec((None, s, width), lambda bi, h, i: (bi, 0, h))
    vtspec = pl.BlockSpec((None, DIFF_DV, s), lambda bi, h, i: (bi, h, 0))
    return pl.pallas_call(
        functools.partial(_attn_prompt_kernel, tq=tq, seq=s, lam_init=lam_init),
        grid=grid,
        in_specs=[_const_spec((4, DIFF_DH)), qspec, kspec, vtspec],
        out_specs=qspec,
        out_shape=jax.ShapeDtypeStruct((b, s, DV_W), F32),
        scratch_shapes=[pltpu.VMEM((DIFF_DV + ONES_ROWS, s), BF16),
                        pltpu.VMEM((width, 2 * tq), BF16),
                        pltpu.VMEM((ATTN_CHAIN, ATTN_SLOTS * ATTN_CHAIN), F32),
                        pltpu.VMEM((1, 2 * tq), F32),
                        pltpu.VMEM((DIFF_DV + ONES_ROWS, 2 * tq), F32)],
        compiler_params=pltpu.CompilerParams(
            dimension_semantics=("arbitrary", "arbitrary", "arbitrary"),
            vmem_limit_bytes=VMEM_LIMIT),
        name="attn_prompt",
    )(lam_params, q, kb, vt)


def _head_norm(o, g):
    return jnp.concatenate(
        [_rms(o[:, h * LANES:(h + 1) * LANES], g) for h in range(o.shape[1] // LANES)], axis=1)


def _ffn_mix(x_ref, go_ref, gg_ref, do_ref, gn_ref, dn_ref, wout_ref, lam_init):
    gg = gg_ref[...]
    go = _head_norm(go_ref[...], gn_ref[...]) * (gg * jax.nn.sigmoid(gg))
    do = _head_norm(do_ref[...], dn_ref[...]) * (1.0 - lam_init)
    mix = jnp.concatenate([go, do], axis=1).astype(BF16)
    return x_ref[...] + _dot(mix, wout_ref[...])


def _ffn_gate_up(h, fn_ref, wg_ref, wu_ref):
    hn = _rms(h, fn_ref[...]).astype(BF16)
    return _dot(hn, wg_ref[...]), _dot(hn, wu_ref[...])


def _ffn_finish(h, gate, up, p_ref, wd_ref, pn_ref, wpg_ref, wpp_ref):
    ple = _dot(p_ref[...].astype(BF16), wpp_ref[...])
    h = h + _dot(((gate * jax.nn.sigmoid(gate)) * up).astype(BF16), wd_ref[...])
    pg = jax.nn.sigmoid(_dot(_rms(h, pn_ref[...]).astype(BF16), wpg_ref[...]))
    return h + pg * ple


def _attn_sample_ffn_kernel(pt_ref, lp_ref, q_ref, kn_ref, vn_ref, ckt_hbm, cv_hbm, x_ref, go_ref,
                            gg_ref, do_ref, p_ref, gn_ref, dn_ref, wout_ref, fn_ref, wg_ref, wu_ref,
                            wd_ref, pn_ref, wpg_ref, wpp_ref, o_ref, y_ref, qs_scr, m_scr, l_scr,
                            acc_scr, kbuf, vbuf, sem, *, pages, t_new, lam_init):
    p = pl.program_id(1)
    per_row = pl.num_programs(1)
    step = pl.program_id(0) * per_row + p
    last_step = pl.num_programs(0) * per_row - 1
    slot = step % 2
    n_rows = 2 * DIFF_HEADS * t_new
    head_rows = 2 * t_new

    def page_copies(of_step, into_slot):
        row = of_step // per_row
        col = (of_step % per_row) * pages
        copies = []
        for g in range(pages):
            page = pt_ref[row, col + g]
            copies.append(pltpu.make_async_copy(ckt_hbm.at[page], kbuf.at[into_slot, g],
                                                sem.at[into_slot, 0, g]))
            copies.append(pltpu.make_async_copy(cv_hbm.at[page], vbuf.at[into_slot, g],
                                                sem.at[into_slot, 1, g]))
        return copies

    @pl.when(step == 0)
    def _():
        for c in page_copies(0, 0):
            c.start()

    for c in page_copies(step, slot):
        c.wait()

    @pl.when(p == 0)
    def _():
        q = q_ref[...] * (DIFF_DH ** -0.5)
        lane = lax.broadcasted_iota(jnp.int32, (t_new, DQ_W), 1) // DIFF_DH
        qs = jnp.concatenate([jnp.where(lane == g, q, 0.0) for g in range(2 * DIFF_HEADS)],
                             axis=0).astype(BF16)
        qs_scr[...] = qs
        s = _dot_nt(qs, kn_ref[...].astype(BF16))
        qpos = lax.broadcasted_iota(jnp.int32, (n_rows, t_new), 0) % t_new
        kpos = lax.broadcasted_iota(jnp.int32, (n_rows, t_new), 1)
        s = jnp.where(kpos <= qpos, s, -jnp.inf)
        m = jnp.max(s, axis=1, keepdims=True)
        e = jnp.exp(s - m)
        m_scr[...] = m
        l_scr[...] = jnp.sum(e, axis=1, keepdims=True)
        eb = e.astype(BF16)
        vn = vn_ref[...].astype(BF16)
        acc_scr[...] = jnp.concatenate(
            [_dot(eb[h * head_rows:(h + 1) * head_rows, :], vn[:, h * DIFF_DV:(h + 1) * DIFF_DV])
             for h in range(DIFF_HEADS)], axis=0)

    h_mid = _ffn_mix(x_ref, go_ref, gg_ref, do_ref, gn_ref, dn_ref, wout_ref, lam_init)
    qs = qs_scr[...]
    s = jnp.concatenate(
        [_dot(qs, jnp.concatenate([kbuf[slot, g], kbuf[slot, g + 1]], axis=1).astype(BF16))
         for g in range(0, pages, 2)], axis=1)
    for c in page_copies(jnp.minimum(step + 1, last_step), 1 - slot):
        c.start()
    gate, up = _ffn_gate_up(h_mid, fn_ref, wg_ref, wu_ref)
    m_old = m_scr[...]
    m_new = jnp.maximum(m_old, jnp.max(s, axis=1, keepdims=True))
    alpha = jnp.exp(m_old - m_new)
    e = jnp.exp(s - m_new)
    l_scr[...] = alpha * l_scr[...] + jnp.sum(e, axis=1, keepdims=True)
    eb = e.astype(BF16)
    pv = []
    for h in range(DIFF_HEADS):
        rows = slice(h * head_rows, (h + 1) * head_rows)
        acc_h = None
        for g in range(0, pages, 2):
            v_h = jnp.concatenate(
                [vbuf[slot, g, pl.ds(h, PAGE_SIZE, stride=DIFF_HEADS), :],
                 vbuf[slot, g + 1, pl.ds(h, PAGE_SIZE, stride=DIFF_HEADS), :]],
                axis=0).astype(BF16)
            t = _dot(eb[rows, g * PAGE_SIZE:(g + 2) * PAGE_SIZE], v_h)
            acc_h = t if acc_h is None else acc_h + t
        pv.append(acc_h)
    y_ref[...] = _ffn_finish(h_mid, gate, up, p_ref, wd_ref, pn_ref, wpg_ref, wpp_ref)
    acc_scr[...] = alpha * acc_scr[...] + jnp.concatenate(pv, axis=0)
    m_scr[...] = m_new

    @pl.when(p == pl.num_programs(1) - 1)
    def _():
        out = acc_scr[...] / l_scr[...]
        lam = _lambda(lp_ref, lam_init)
        heads = []
        for h in range(DIFF_HEADS):
            r1 = h * head_rows
            r2 = r1 + t_new
            heads.append(out[r1:r1 + t_new, :] - lam * out[r2:r2 + t_new, :])
        o_ref[...] = jnp.concatenate(heads, axis=1)

    @pl.when(step == last_step)
    def _():
        for c in page_copies(last_step, 1 - slot):
            c.wait()


def _ffn_operands(x, go, gg, do, p, w):
    return (x, go, gg, do, p, w["gla_norm_g"], w["diff_norm_g"], w["w_out"], w["ffn_norm_g"],
            w["w_ffn_gate"], w["w_ffn_up"], w["w_ffn_down"], w["ple_norm_g"], w["w_ple_gate"],
            w["w_ple_proj"])


def _ffn_in_specs(tok, ple, d_ff):
    return [tok(D_MODEL), tok(GV_W), tok(GV_W), tok(DV_W), tok(ple),
            _const_spec((1, GLA_DV)), _const_spec((1, DIFF_DV)),
            _const_spec((D_MODEL, D_MODEL)), _const_spec((1, D_MODEL)),
            _const_spec((D_MODEL, d_ff)), _const_spec((D_MODEL, d_ff)),
            _const_spec((d_ff, D_MODEL)), _const_spec((1, D_MODEL)),
            _const_spec((D_MODEL, D_MODEL)), _const_spec((ple, D_MODEL))]


def _attn_sample_ffn(q, k_new, v_new, cache_kt, cache_v, page_table, lam_params, lam_init, ffn, *,
                     pages):
    db, t_new, _ = q.shape
    n_pages = page_table.shape[1]
    per_row = n_pages // pages
    assert per_row * pages == n_pages and pages % 2 == 0
    grid = (db, per_row)
    n_rows = 2 * DIFF_HEADS * t_new
    x, w = ffn[0], ffn[5]
    n = x.shape[0]
    tm = n // (db * per_row)
    assert tm * db * per_row == n and tm % 8 == 0
    tok = pl.BlockSpec((None, t_new, DQ_W), lambda b, p, pt: (b, 0, 0))
    ftok = lambda width: pl.BlockSpec((tm, width), lambda b, p, pt: (b * per_row + p, 0))
    hbm = pl.BlockSpec(memory_space=pl.ANY)
    grid_spec = pltpu.PrefetchScalarGridSpec(
        num_scalar_prefetch=1,
        grid=grid,
        in_specs=[pl.BlockSpec((4, DIFF_DH), lambda b, p, pt: (0, 0)), tok, tok, tok, hbm, hbm]
        + _ffn_in_specs(ftok, ffn[4].shape[1], w["w_ffn_gate"].shape[1]),
        out_specs=[tok, ftok(D_MODEL)],
        scratch_shapes=[pltpu.VMEM((n_rows, DQ_W), BF16), pltpu.VMEM((n_rows, 1), F32),
                        pltpu.VMEM((n_rows, 1), F32), pltpu.VMEM((n_rows, DIFF_DV), F32),
                        pltpu.VMEM((2, pages, DQ_W, PAGE_SIZE), F32),
                        pltpu.VMEM((2, pages, PAGE_SIZE * DIFF_HEADS, DIFF_DV), F32),
                        pltpu.SemaphoreType.DMA((2, 2, pages))],
    )
    return pl.pallas_call(
        functools.partial(_attn_sample_ffn_kernel, pages=pages, t_new=t_new, lam_init=lam_init),
        grid_spec=grid_spec,
        out_shape=[jax.ShapeDtypeStruct((db, t_new, DV_W), F32),
                   jax.ShapeDtypeStruct((n, D_MODEL), F32)],
        compiler_params=pltpu.CompilerParams(dimension_semantics=("arbitrary", "arbitrary"),
                                             vmem_limit_bytes=VMEM_LIMIT),
        name="attn_sample_ffn",
    )(page_table, lam_params, q, k_new, v_new, cache_kt, cache_v, *_ffn_operands(*ffn))


def _out_ffn_kernel(x_ref, go_ref, gg_ref, do_ref, p_ref, gn_ref, dn_ref, wout_ref, fn_ref,
                    wg_ref, wu_ref, wd_ref, pn_ref, wpg_ref, wpp_ref, y_ref, *, lam_init):
    h = _ffn_mix(x_ref, go_ref, gg_ref, do_ref, gn_ref, dn_ref, wout_ref, lam_init)
    gate, up = _ffn_gate_up(h, fn_ref, wg_ref, wu_ref)
    y_ref[...] = _ffn_finish(h, gate, up, p_ref, wd_ref, pn_ref, wpg_ref, wpp_ref)


def _out_ffn(x, go, gg, do, p, w, lam_init, *, tm):
    n = x.shape[0]
    tok = lambda width: pl.BlockSpec((tm, width), lambda i: (i, 0))
    return pl.pallas_call(
        functools.partial(_out_ffn_kernel, lam_init=lam_init),
        grid=(n // tm,),
        in_specs=_ffn_in_specs(tok, p.shape[1], w["w_ffn_gate"].shape[1]),
        out_specs=tok(D_MODEL),
        out_shape=jax.ShapeDtypeStruct((n, D_MODEL), F32),
        compiler_params=pltpu.CompilerParams(dimension_semantics=("arbitrary",),
                                             vmem_limit_bytes=VMEM_LIMIT),
        name="out_ffn",
    )(*_ffn_operands(x, go, gg, do, p, w))


def _rope_tables(pos):
    half = DIFF_DH // 2
    inv = ROPE_THETA ** (-jnp.arange(half, dtype=F32) / half)
    ang = pos.astype(F32)[:, None] * inv[None, :]
    cos, sin = jnp.cos(ang), jnp.sin(ang)
    cos = jnp.concatenate([cos, cos], axis=1)
    sin = jnp.concatenate([-sin, sin], axis=1)
    reps = LANES // DIFF_DH
    return jnp.tile(cos, (1, reps)), jnp.tile(sin, (1, reps))


def _layer_weights(i, attn_norm_g, w_in, w_gk2, b_gk, q_norm_g, k_norm_g, lam_params, gla_norm_g,
                   diff_norm_g, w_out, ffn_norm_g, w_ffn_gate, w_ffn_up, w_ffn_down, ple_norm_g,
                   w_ple_gate, w_ple_proj):
    lr0 = 2 * GK_W + 2 * GV_W
    wi = w_in[i]
    w_lr = jnp.pad(wi[:, lr0:lr0 + GLA_GATE_RANK], ((0, 0), (0, LR_PAD - GLA_GATE_RANK)))
    gid = jnp.arange(LANES) // DIFF_DH
    dq0 = lr0 + GLA_GATE_RANK
    row = lambda a: a.reshape(1, -1).astype(F32)
    return {
        "attn_norm_g": row(attn_norm_g[i]),
        "w_proj": jnp.concatenate([wi[:, dq0:dq0 + 2 * DQ_W], wi[:, :lr0], wi[:, dq0 + 2 * DQ_W:]],
                                  axis=1).astype(BF16),
        "w_lr": w_lr.astype(BF16),
        "w_gk2": jnp.pad(w_gk2[i], ((0, LR_PAD - GLA_GATE_RANK), (0, 0))).astype(BF16),
        "b_gk": row(b_gk[i]),
        "q_norm_g": row(jnp.tile(q_norm_g[i], DQ_W // DIFF_DH)),
        "k_norm_g": row(jnp.tile(k_norm_g[i], DQ_W // DIFF_DH)),
        "group_ind": (gid[:, None] == gid[None, :]).astype(BF16),
        "lam_params": lam_params[i].astype(F32),
        "gla_norm_g": row(gla_norm_g[i]),
        "diff_norm_g": row(diff_norm_g[i]),
        "w_out": w_out[i].astype(BF16),
        "ffn_norm_g": row(ffn_norm_g[i]),
        "w_ffn_gate": w_ffn_gate[i].astype(BF16),
        "w_ffn_up": w_ffn_up[i].astype(BF16),
        "w_ffn_down": w_ffn_down[i].astype(BF16),
        "ple_norm_g": row(ple_norm_g[i]),
        "w_ple_gate": w_ple_gate[i].astype(BF16),
        "w_ple_proj": w_ple_proj[i].astype(BF16),
    }


def kernel(x_prompt, x_sample, p_prompt, p_sample, cache_k, cache_v, state_gla, page_table, attn_norm_g, w_in, w_gk2, b_gk, q_norm_g, k_norm_g, lam_params, gla_norm_g, diff_norm_g, w_out, ffn_norm_g, w_ffn_gate, w_ffn_up, w_ffn_down, ple_norm_g, w_ple_gate, w_ple_proj):
    B, S, _ = x_prompt.shape
    DB, T, _ = x_sample.shape
    depth = w_in.shape[0]
    n_pool = cache_k.shape[1]
    n_pages = page_table.shape[1]
    past_len = n_pages * PAGE_SIZE

    cos_p, sin_p = _rope_tables(jnp.arange(S))
    cos_s, sin_s = _rope_tables(jnp.tile(past_len + jnp.arange(T), DB))
    t_pad = GLA_GROUP
    cache_k2 = jnp.transpose(cache_k, (0, 1, 3, 4, 5, 2)).reshape(depth * n_pool, DQ_W, PAGE_SIZE)
    cache_v2 = cache_v.reshape(depth * n_pool, PAGE_SIZE * DIFF_HEADS, DIFF_DV)

    hp = x_prompt.reshape(B * S, D_MODEL)
    hs = x_sample.reshape(DB * T, D_MODEL)
    kp_l, vp_l, gp_l, ks_l, vs_l, gs_l = [], [], [], [], [], []
    for i in range(depth):
        lam_init = 0.8 - 0.6 * math.exp(-0.3 * i)
        w = _layer_weights(i, attn_norm_g, w_in, w_gk2, b_gk, q_norm_g, k_norm_g, lam_params,
                           gla_norm_g, diff_norm_g, w_out, ffn_norm_g, w_ffn_gate, w_ffn_up,
                           w_ffn_down, ple_norm_g, w_ple_gate, w_ple_proj)

        gq, gk, gv, gg, la, dq, dkb, dkt, dv4, dvt = _proj(hp, cos_p, sin_p, w, tm=PROJ_TM,
                                                           pos_period_blocks=S // PROJ_TM, seq=S)
        r3 = lambda a: a.reshape(B, S, a.shape[-1])
        s0 = jnp.zeros((B, GK_W, GLA_DV), F32)
        gla_o, gla_s = _gla(r3(gq), r3(gk), r3(gv), r3(la), s0, block=GLA_BLOCK)
        diff_o = _attn_prompt(r3(dq), r3(dkb), dvt, w["lam_params"], lam_init, tq=ATTN_TQ)
        ffn_prompt = (hp, gla_o.reshape(B * S, GV_W), gg, diff_o.reshape(B * S, DV_W),
                      p_prompt[i].reshape(B * S, -1), w)
        kp_l.append(jnp.transpose(dkt.reshape(B, DIFF_HEADS, 2, DIFF_DH, S), (0, 4, 1, 2, 3)))
        vp_l.append(dv4.reshape(B, S, DIFF_HEADS, DIFF_DV))
        gp_l.append(gla_s.reshape(B, GLA_HEADS, GLA_DK, GLA_DV))

        gq, gk, gv, gg, la, dq, dk, dv = _proj(hs, cos_s, sin_s, w, tm=DB * T, pos_period_blocks=1)
        r3 = lambda a: a.reshape(DB, T, a.shape[-1])
        padt = lambda a: jnp.pad(r3(a), ((0, 0), (0, t_pad - T), (0, 0)))
        s0 = state_gla[i].reshape(DB, GK_W, GLA_DV)
        gla_o, gla_s = _gla(padt(gq), padt(gk), padt(gv), padt(la), s0, block=t_pad)
        gla_o = gla_o[:, :T]
        diff_o, hp = _attn_sample_ffn(r3(dq), r3(dk), r3(dv), cache_k2, cache_v2,
                                      page_table + i * n_pool, w["lam_params"], lam_init,
                                      ffn_prompt, pages=PAGES_PER_STEP)
        hs = _out_ffn(hs, gla_o.reshape(DB * T, GV_W), gg, diff_o.reshape(DB * T, DV_W),
                      p_sample[i].reshape(DB * T, -1), w, lam_init, tm=DB * T)
        ks_l.append(dk.reshape(DB, T, DIFF_HEADS, 2, DIFF_DH))
        vs_l.append(dv.reshape(DB, T, DIFF_HEADS, DIFF_DV))
        gs_l.append(gla_s.reshape(DB, GLA_HEADS, GLA_DK, GLA_DV))

    return (hp.reshape(B, S, D_MODEL), hs.reshape(DB, T, D_MODEL),
            jnp.stack(kp_l), jnp.stack(vp_l), jnp.stack(gp_l),
            jnp.stack(ks_l), jnp.stack(vs_l), jnp.stack(gs_l))
```
